```python
import math
import jax, jax.numpy as jnp
from jax import lax
import numpy as np


D_MODEL = 2048
BATCH = 4
SEQ = 2048
DEPTH = 4
DEC_BATCH = 128
DEC_SEQ = 8
PAST_LEN = 16384
PAGE_SIZE = 128

A_WIDTH = D_MODEL // 4
A_GROUPS = 4
A_GW = A_WIDTH // A_GROUPS
CHUNK = 128
B_WIDTH = D_MODEL // 2
B_HEAD = 64
B_HEADS = B_WIDTH // B_HEAD
B_DECAY_LORA = 64
B_AAA_LORA = 64
B_GATE_LORA = 160
B_COLS = 3 * B_WIDTH + B_DECAY_LORA + B_AAA_LORA + B_GATE_LORA
RWKV_GN_EPS = 64e-5
C_WIDTH = D_MODEL // 4
CONV_W = 31
N_BRANCH = 3
A_COLS = 2 * A_WIDTH
C_COLS = 2 * C_WIDTH
G_COLS = N_BRANCH * D_MODEL
IN_COLS = A_COLS + B_COLS + C_COLS + G_COLS
N_MEM = 256
X_HEADS = 4
X_HEAD_DIM = 128
X_WIDTH = X_HEADS * X_HEAD_DIM
D_FF = 5632
EPS = 1e-6
LN_EPS = 1e-5

kernel_name = 'hybrid_gmlp_rwkv7_conformer_decode_step'


def rmsnorm(x, g):
    xf = x.astype(jnp.float32)
    y = xf * lax.rsqrt(jnp.mean(xf * xf, axis=-1, keepdims=True) + EPS)
    return (y * g.astype(jnp.float32)).astype(x.dtype)


def layernorm(x, g, b):
    xf = x.astype(jnp.float32)
    xc = xf - jnp.mean(xf, axis=-1, keepdims=True)
    y = xc * lax.rsqrt(jnp.mean(xc * xc, axis=-1, keepdims=True) + LN_EPS)
    return (y * g.astype(jnp.float32) + b.astype(jnp.float32)).astype(x.dtype)


def swiglu(h, w1, w3, w2):
    return (jax.nn.silu(h @ w1) * (h @ w3)) @ w2


def chunk_spatial_gate(u, v_n, w_s, b_s):
    bn, t_len, _ = v_n.shape
    n_chunks = -(-t_len // CHUNK)
    pad = n_chunks * CHUNK - t_len
    vp = jnp.pad(v_n, ((0, 0), (0, pad), (0, 0))).reshape(bn, n_chunks, CHUNK, A_GROUPS, A_GW)
    mask = jnp.tril(jnp.ones((CHUNK, CHUNK), dtype=bool))
    ws = jnp.where(mask[None], w_s, jnp.zeros_like(w_s)).astype(vp.dtype)
    mixed = jnp.einsum('gts,bnsgc->bntgc', ws, vp) + b_s.T.astype(vp.dtype)[None, None, :, :, None]
    mixed = mixed.reshape(bn, n_chunks * CHUNK, A_WIDTH)[:, :t_len]
    return u * mixed


def gmlp_branch(z_a, p):
    z_a = jax.nn.gelu(z_a)
    u, v = jnp.split(z_a, [A_WIDTH], axis=-1)
    v_n = layernorm(v, p['ln_a_g'], p['ln_a_b'])
    y = chunk_spatial_gate(u, v_n, p['w_s'], p['b_s']) @ p['wo_a']
    return y, v_n


def rwkv7_scan(r, w, k, v, kk, a, s0):
    def step(s, inp):
        r_t, w_t, k_t, v_t, kk_t, a_t = inp
        sa = jnp.einsum('bhvk,bhk->bhv', s, -kk_t)
        s = (s * w_t[:, :, None, :]
             + sa[..., None] * (kk_t * a_t)[:, :, None, :]
             + v_t[..., None] * k_t[:, :, None, :])
        return s, jnp.einsum('bhvk,bhk->bhv', s, r_t)
    xs = tuple(jnp.moveaxis(t, 1, 0) for t in (r, w, k, v, kk, a))
    s_fin, out = lax.scan(step, s0, xs)
    return jnp.moveaxis(out, 0, 1), s_fin


def rwkv_branch(z_b, shift_row, s0, p):
    f32 = jnp.float32
    bn, t_len, _ = z_b.shape
    prev = jnp.concatenate([shift_row[:, None].astype(z_b.dtype), z_b[:, :-1]], axis=1)
    zs = z_b + (prev - z_b) * p['mu_b']
    idx = [B_WIDTH, 2 * B_WIDTH, 3 * B_WIDTH, 3 * B_WIDTH + B_DECAY_LORA,
           3 * B_WIDTH + B_DECAY_LORA + B_AAA_LORA]
    r, k, v, zw, za, zg = jnp.split(zs, idx, axis=-1)
    w_raw = -jax.nn.softplus(-(p['w0'] + jnp.tanh(zw) @ p['w2_decay']).astype(f32)) - 0.5
    decay = jnp.exp(-jnp.exp(w_raw))
    a = jax.nn.sigmoid((p['a0'] + za @ p['a2_aaa']).astype(f32))
    g = jax.nn.sigmoid(zg) @ p['g2_gate']
    heads = lambda t: t.reshape(bn, t_len, B_HEADS, B_HEAD)
    kf = k.astype(f32)
    kk = heads(kf * p['k_k'].astype(f32))
    kk = kk / jnp.maximum(jnp.sqrt(jnp.sum(kk * kk, axis=-1, keepdims=True)), 1e-12)
    k_mod = heads(kf * (1.0 + (a - 1.0) * p['k_a'].astype(f32)))
    rh = heads(r.astype(f32))
    vh = heads(v.astype(f32))
    out, s_fin = rwkv7_scan(rh, heads(decay), k_mod, vh, kk, heads(a), s0.astype(f32))
    mean = jnp.mean(out, axis=-1, keepdims=True)
    oc = out - mean
    on = oc * lax.rsqrt(jnp.mean(oc * oc, axis=-1, keepdims=True) + RWKV_GN_EPS)
    on = on.reshape(bn, t_len, B_WIDTH) * p['gn_g'].astype(f32) + p['gn_b'].astype(f32)
    bonus = jnp.sum(rh * k_mod * p['r_k'].astype(f32), axis=-1, keepdims=True) * vh
    on = on + bonus.reshape(bn, t_len, B_WIDTH)
    y = (on.astype(z_b.dtype) * g) @ p['wo_b']
    return y, s_fin, z_b[:, -1]


def conv_branch(z_c, conv_buf, p):
    val, gate = jnp.split(z_c, [C_WIDTH], axis=-1)
    glu = val * jax.nn.sigmoid(gate)
    ext = jnp.concatenate([conv_buf.astype(glu.dtype), glu], axis=1)
    y = lax.conv_general_dilated(
        ext, p['dw'][:, None, :].astype(ext.dtype), window_strides=(1,), padding='VALID',
        dimension_numbers=('NWC', 'WIO', 'NWC'), feature_group_count=C_WIDTH)
    y = y + p['dw_b']
    y = jax.nn.silu(layernorm(y, p['ln_c_g'], p['ln_c_b'])) @ p['wo_c']
    return y, ext[:, -(CONV_W - 1):]


def mem_kv(mem, g, wk, wv):
    m = rmsnorm(mem, g)
    bn = mem.shape[0]
    k = (m @ wk).reshape(bn, N_MEM, X_HEADS, X_HEAD_DIM)
    v = (m @ wv).reshape(bn, N_MEM, X_HEADS, X_HEAD_DIM)
    return k, v


def cross_attend(h, mem_k, mem_v, wq, wo):
    bn, t_len, _ = h.shape
    q = (h @ wq).reshape(bn, t_len, X_HEADS, X_HEAD_DIM)
    s = jnp.einsum('bthd,bmhd->bhtm', q, mem_k.astype(q.dtype)).astype(jnp.float32) * (X_HEAD_DIM ** -0.5)
    pr = jax.nn.softmax(s, axis=-1).astype(h.dtype)
    o = jnp.einsum('bhtm,bmhd->bthd', pr, mem_v.astype(h.dtype)).reshape(bn, t_len, X_WIDTH)
    return o @ wo


def trunk_layer(x, p, mem_k, mem_v, shift_row, rwkv_state, conv_buf):
    x = x + 0.5 * swiglu(rmsnorm(x, p['n_ffn1']), p['ffn1_w1'], p['ffn1_w3'], p['ffn1_w2'])
    z = rmsnorm(x, p['n_mix']) @ p['w_in']
    z_a, z_b, z_c, z_g = jnp.split(z, [A_COLS, A_COLS + B_COLS, A_COLS + B_COLS + C_COLS], axis=-1)
    y_a, v_rows = gmlp_branch(z_a, p)
    y_b, new_rwkv, new_shift = rwkv_branch(z_b, shift_row, rwkv_state, p)
    y_c, new_buf = conv_branch(z_c, conv_buf, p)
    gates = jax.nn.sigmoid(z_g).reshape(*z_g.shape[:-1], N_BRANCH, D_MODEL)
    merged = gates[..., 0, :] * y_a + gates[..., 1, :] * y_b + gates[..., 2, :] * y_c
    x = x + merged @ p['w_out']
    x = x + cross_attend(rmsnorm(x, p['n_x']), mem_k, mem_v, p['w_xq'], p['w_xo'])
    x = x + 0.5 * swiglu(rmsnorm(x, p['n_ffn2']), p['ffn2_w1'], p['ffn2_w3'], p['ffn2_w2'])
    return x, v_rows, new_rwkv, new_shift, new_buf


def setup_inputs(seed: int = 0) -> dict:
    key = jax.random.key(seed)
    ks = iter(jax.random.split(key, 64))
    nrm = lambda shape, scale: scale * jax.random.normal(next(ks), shape, jnp.float32)
    gain = lambda shape: 1.0 + 0.02 * jax.random.normal(next(ks), shape, jnp.float32)
    L = DEPTH
    return {
        'x_prompt': nrm((BATCH, SEQ, D_MODEL), 1.0),
        'x_sample': nrm((DEC_BATCH, DEC_SEQ, D_MODEL), 1.0),
        'mem_prompt': nrm((BATCH, N_MEM, D_MODEL), 1.0),
        'state_rwkv': nrm((L, DEC_BATCH, B_HEADS, B_HEAD, B_HEAD), 0.3),
        'state_shift': nrm((L, DEC_BATCH, B_COLS), 1.0),
        'state_conv': nrm((L, DEC_BATCH, CONV_W - 1, C_WIDTH), 0.5),
        'cache_mem_k': nrm((L, DEC_BATCH, N_MEM, X_HEADS, X_HEAD_DIM), 1.0),
        'cache_mem_v': nrm((L, DEC_BATCH, N_MEM, X_HEADS, X_HEAD_DIM), 1.0),
        'n_ffn1': gain((L, D_MODEL)),
        'ffn1_w1': nrm((L, D_MODEL, D_FF), D_MODEL ** -0.5),
        'ffn1_w3': nrm((L, D_MODEL, D_FF), D_MODEL ** -0.5),
        'ffn1_w2': nrm((L, D_FF, D_MODEL), D_FF ** -0.5),
        'n_mix': gain((L, D_MODEL)),
        'w_in': nrm((L, D_MODEL, IN_COLS), D_MODEL ** -0.5),
        'mu_b': jax.random.uniform(next(ks), (L, B_COLS), jnp.float32),
        'w0': -1.0 + nrm((L, B_WIDTH), 0.5),
        'w2_decay': nrm((L, B_DECAY_LORA, B_WIDTH), 0.5 * B_DECAY_LORA ** -0.5),
        'a0': nrm((L, B_WIDTH), 0.5),
        'a2_aaa': nrm((L, B_AAA_LORA, B_WIDTH), 0.5 * B_AAA_LORA ** -0.5),
        'g2_gate': nrm((L, B_GATE_LORA, B_WIDTH), B_GATE_LORA ** -0.5),
        'k_k': 0.85 + nrm((L, B_WIDTH), 0.1),
        'k_a': 1.0 + nrm((L, B_WIDTH), 0.1),
        'r_k': nrm((L, B_HEADS, B_HEAD), 0.1),
        'gn_g': gain((L, B_WIDTH)),
        'gn_b': nrm((L, B_WIDTH), 0.02),
        'wo_b': nrm((L, B_WIDTH, D_MODEL), B_WIDTH ** -0.5),
        'ln_a_g': gain((L, A_WIDTH)),
        'ln_a_b': nrm((L, A_WIDTH), 0.02),
        'w_s': nrm((L, A_GROUPS, CHUNK, CHUNK), CHUNK ** -0.5),
        'b_s': 1.0 + nrm((L, A_GROUPS, CHUNK), 0.1),
        'wo_a': nrm((L, A_WIDTH, D_MODEL), A_WIDTH ** -0.5),
        'dw': nrm((L, CONV_W, C_WIDTH), CONV_W ** -0.5),
        'dw_b': nrm((L, C_WIDTH), 0.02),
        'ln_c_g': gain((L, C_WIDTH)),
        'ln_c_b': nrm((L, C_WIDTH), 0.02),
        'wo_c': nrm((L, C_WIDTH, D_MODEL), C_WIDTH ** -0.5),
        'w_out': nrm((L, D_MODEL, D_MODEL), D_MODEL ** -0.5),
        'n_x': gain((L, D_MODEL)),
        'n_mem': gain((L, D_MODEL)),
        'w_xq': nrm((L, D_MODEL, X_WIDTH), D_MODEL ** -0.5),
        'w_xk': nrm((L, D_MODEL, X_WIDTH), D_MODEL ** -0.5),
        'w_xv': nrm((L, D_MODEL, X_WIDTH), D_MODEL ** -0.5),
        'w_xo': nrm((L, X_WIDTH, D_MODEL), X_WIDTH ** -0.5),
        'n_ffn2': gain((L, D_MODEL)),
        'ffn2_w1': nrm((L, D_MODEL, D_FF), D_MODEL ** -0.5),
        'ffn2_w3': nrm((L, D_MODEL, D_FF), D_MODEL ** -0.5),
        'ffn2_w2': nrm((L, D_FF, D_MODEL), D_FF ** -0.5),
        'n_final': gain((D_MODEL,)),
    }


def reference(x_prompt, x_sample, mem_prompt, state_rwkv, state_shift, state_conv,
              cache_mem_k, cache_mem_v,
              n_ffn1, ffn1_w1, ffn1_w3, ffn1_w2, n_mix, w_in, mu_b, w0, w2_decay, a0, a2_aaa,
              g2_gate, k_k, k_a, r_k, gn_g, gn_b, wo_b, ln_a_g, ln_a_b, w_s, b_s, wo_a,
              dw, dw_b, ln_c_g, ln_c_b, wo_c, w_out, n_x, n_mem, w_xq, w_xk, w_xv, w_xo,
              n_ffn2, ffn2_w1, ffn2_w3, ffn2_w2, n_final):
    stacked = dict(
        n_ffn1=n_ffn1, ffn1_w1=ffn1_w1, ffn1_w3=ffn1_w3, ffn1_w2=ffn1_w2, n_mix=n_mix, w_in=w_in,
        mu_b=mu_b, w0=w0, w2_decay=w2_decay, a0=a0, a2_aaa=a2_aaa, g2_gate=g2_gate, k_k=k_k,
        k_a=k_a, r_k=r_k, gn_g=gn_g, gn_b=gn_b, wo_b=wo_b, ln_a_g=ln_a_g, ln_a_b=ln_a_b,
        w_s=w_s, b_s=b_s, wo_a=wo_a, dw=dw, dw_b=dw_b, ln_c_g=ln_c_g, ln_c_b=ln_c_b, wo_c=wo_c,
        w_out=w_out, n_x=n_x, n_mem=n_mem, w_xq=w_xq, w_xk=w_xk, w_xv=w_xv, w_xo=w_xo,
        n_ffn2=n_ffn2, ffn2_w1=ffn2_w1, ffn2_w3=ffn2_w3, ffn2_w2=ffn2_w2)
    bp = x_prompt.shape[0]
    xp, xs = x_prompt, x_sample
    rwkv_p, shift_p, conv_p, mk_p, mv_p = [], [], [], [], []
    rwkv_s, shift_s, conv_s, gv_s = [], [], [], []
    for l in range(DEPTH):
        p = {name: arr[l] for name, arr in stacked.items()}
        mk, mv = mem_kv(mem_prompt, p['n_mem'], p['w_xk'], p['w_xv'])
        xp, _, s_p, row_p, buf_p = trunk_layer(
            xp, p, mk, mv,
            jnp.zeros((bp, B_COLS), xp.dtype),
            jnp.zeros((bp, B_HEADS, B_HEAD, B_HEAD), jnp.float32),
            jnp.zeros((bp, CONV_W - 1, C_WIDTH), xp.dtype))
        rwkv_p.append(s_p.astype(state_rwkv.dtype))
        shift_p.append(row_p)
        conv_p.append(buf_p)
        mk_p.append(mk)
        mv_p.append(mv)
        xs, v_rows, s_s, row_s, buf_s = trunk_layer(
            xs, p, cache_mem_k[l], cache_mem_v[l], state_shift[l], state_rwkv[l], state_conv[l])
        rwkv_s.append(s_s.astype(state_rwkv.dtype))
        shift_s.append(row_s)
        conv_s.append(buf_s)
        gv_s.append(v_rows)
    y_prompt = rmsnorm(xp, n_final)
    y_sample = rmsnorm(xs, n_final)
    return (y_prompt, y_sample,
            jnp.stack(rwkv_p), jnp.stack(shift_p), jnp.stack(conv_p), jnp.stack(mk_p), jnp.stack(mv_p),
            jnp.stack(rwkv_s), jnp.stack(shift_s), jnp.stack(conv_s), jnp.stack(gv_s))
```

```python
import functools
import math

import jax
import jax.numpy as jnp
from jax import lax
from jax.experimental import pallas as pl
from jax.experimental.pallas import tpu as pltpu

F32 = jnp.float32
BF16 = jnp.bfloat16

LANES = 128
SUBLANES = 8
VMEM_LIMIT_BYTES = 56 * 1024 * 1024

D_MODEL = 2048
A_WIDTH = 512
A_GROUPS = 4
CHUNK = 128
B_WIDTH = 1024
B_HEAD = 64
B_HEADS = 16
B_PAIRS = B_HEADS // 2
B_LORA = 64 + 64 + 160
B_LORA_PAD = 384
B_COLS = 3 * B_WIDTH + B_LORA
B_COLS_PAD = 3 * B_WIDTH + B_LORA_PAD
RWKV_GN_EPS = 64e-5
RWKV_CHUNK = 64
C_WIDTH = 512
CONV_W = 31
CONV_HALO = 32
N_MEM = 256
X_HEADS = 4
X_HEAD_DIM = 128
X_WIDTH = 512
D_FF = 5632
EPS = 1e-6
LN_EPS = 1e-5

ROW_TILE = 512


def _params(*sem):
    return pltpu.CompilerParams(dimension_semantics=sem, vmem_limit_bytes=VMEM_LIMIT_BYTES)


def _rms(x, g):
    return x * lax.rsqrt(jnp.mean(x * x, axis=-1, keepdims=True) + EPS) * g


def _layernorm(x, g, b):
    xc = x - jnp.mean(x, axis=-1, keepdims=True)
    return xc * lax.rsqrt(jnp.mean(xc * xc, axis=-1, keepdims=True) + LN_EPS) * g + b


def _dot(a, b):
    return jnp.dot(a, b, preferred_element_type=F32)


def _dot_nt(a, b):
    return lax.dot_general(a, b, (((1,), (1,)), ((), ())), preferred_element_type=F32)


def _split2(x):
    hi = x.astype(BF16)
    lo = (x - hi.astype(F32)).astype(BF16)
    return hi, lo


def _split3(x):
    hi = x.astype(BF16)
    r1 = x - hi.astype(F32)
    mid = r1.astype(BF16)
    lo = (r1 - mid.astype(F32)).astype(BF16)
    return hi, mid, lo


def _dot3(a, b, dot=_dot):
    ah, al = _split2(a)
    bh, bl = _split2(b)
    return dot(ah, bh) + (dot(ah, bl) + dot(al, bh))


def _dot_sel(sel, x, terms=3):
    sel = sel.astype(BF16)
    parts = _split3(x) if terms == 3 else _split2(x)
    acc = _dot(sel, parts[0])
    for p in parts[1:]:
        acc = acc + _dot(sel, p)
    return acc


def _ffn_kernel(x_ref, g_ref, w1_ref, w3_ref, w2_ref, o_ref, h_sc, acc_sc):
    j = pl.program_id(1)

    @pl.when(j == 0)
    def _():
        h_sc[...] = _rms(x_ref[...], g_ref[...]).astype(BF16)
        acc_sc[...] = jnp.zeros_like(acc_sc)

    h = h_sc[...]
    a = _dot(h, w1_ref[...])
    b = _dot(h, w3_ref[...])
    act = (a * jax.nn.sigmoid(a) * b).astype(BF16)
    acc_sc[...] += _dot(act, w2_ref[...])

    @pl.when(j == pl.num_programs(1) - 1)
    def _():
        o_ref[...] = x_ref[...] + 0.5 * acc_sc[...]


def _ffn(x, g, w1, w3, w2, tf=512):
    n, d = x.shape
    f = w1.shape[1]
    tm = ROW_TILE
    return pl.pallas_call(
        _ffn_kernel,
        out_shape=jax.ShapeDtypeStruct((n, d), F32),
        grid=(n // tm, f // tf),
        in_specs=[
            pl.BlockSpec((tm, d), lambda i, j: (i, 0)),
            pl.BlockSpec((1, d), lambda i, j: (0, 0)),
            pl.BlockSpec((d, tf), lambda i, j: (0, j)),
            pl.BlockSpec((d, tf), lambda i, j: (0, j)),
            pl.BlockSpec((tf, d), lambda i, j: (j, 0)),
        ],
        out_specs=pl.BlockSpec((tm, d), lambda i, j: (i, 0)),
        scratch_shapes=[pltpu.VMEM((tm, d), BF16), pltpu.VMEM((tm, d), F32)],
        compiler_params=_params("parallel", "arbitrary"),
        name="ffn",
    )(x, g, w1, w3, w2)


def _norm_mm_kernel(x_ref, g_ref, w_ref, o_ref, h_sc):
    @pl.when(pl.program_id(1) == 0)
    def _():
        h_sc[...] = _rms(x_ref[...], g_ref[...]).astype(BF16)

    o_ref[...] = _dot(h_sc[...], w_ref[...])


def _norm_mm(x, g, w, tn):
    n, d = x.shape
    cols = w.shape[1]
    tm = min(ROW_TILE, n)
    return pl.pallas_call(
        _norm_mm_kernel,
        out_shape=jax.ShapeDtypeStruct((n, cols), F32),
        grid=(n // tm, cols // tn),
        in_specs=[
            pl.BlockSpec((tm, d), lambda i, j: (i, 0)),
            pl.BlockSpec((1, d), lambda i, j: (0, 0)),
            pl.BlockSpec((d, tn), lambda i, j: (0, j)),
        ],
        out_specs=pl.BlockSpec((tm, tn), lambda i, j: (i, j)),
        scratch_shapes=[pltpu.VMEM((tm, d), BF16)],
        compiler_params=_params("parallel", "arbitrary"),
        name="norm_mm",
    )(x, g, w)


def _gmlp_kernel(x_ref, g_ref, w_ref, lng_ref, lnb_ref, ws_ref, bs_ref, act_ref, vn_ref, *,
                 n_prompt_tiles, short_len):
    is_sample = pl.program_id(0) >= n_prompt_tiles
    h = _rms(x_ref[...], g_ref[...]).astype(BF16)
    z = jax.nn.gelu(_dot(h, w_ref[...]))
    u = z[:, :A_WIDTH]
    v_n = _layernorm(z[:, A_WIDTH:], lng_ref[...], lnb_ref[...])
    vn_ref[...] = v_n

    row = lax.broadcasted_iota(jnp.int32, (CHUNK, CHUNK), 0)
    col = lax.broadcasted_iota(jnp.int32, (CHUNK, CHUNK), 1)
    shift = jnp.where(is_sample, int(math.log2(short_len)), int(math.log2(CHUNK)))
    shift = jnp.broadcast_to(shift.astype(jnp.int32), (CHUNK, CHUNK))
    same_seq = lax.shift_right_logical(row, shift) == lax.shift_right_logical(col, shift)
    keep = (col <= row) & same_seq
    v_b = v_n.astype(BF16)
    tm = x_ref.shape[0]
    gw = A_WIDTH // A_GROUPS
    for grp in range(A_GROUPS):
        ws = jnp.where(keep, ws_ref[0, grp], 0.0).astype(BF16)
        bias = bs_ref[0, grp]
        for c in range(tm // CHUNK):
            rows = slice(c * CHUNK, (c + 1) * CHUNK)
            cols = slice(grp * gw, (grp + 1) * gw)
            mixed = _dot(ws, v_b[rows, cols]) + bias
            act_ref[rows, cols] = (u[rows, cols] * mixed).astype(BF16)


def _gmlp(x, g, w_a, ln_g, ln_b, ws_all, bs_all, n_prompt, short_len):
    n, d = x.shape
    tm = ROW_TILE
    npt = n_prompt // tm
    kern = functools.partial(_gmlp_kernel, n_prompt_tiles=npt, short_len=short_len)
    sel = lambda i: (jnp.where(i >= npt, 1, 0), 0, 0, 0)
    return pl.pallas_call(
        kern,
        out_shape=(jax.ShapeDtypeStruct((n, A_WIDTH), BF16), jax.ShapeDtypeStruct((n, A_WIDTH), F32)),
        grid=(n // tm,),
        in_specs=[
            pl.BlockSpec((tm, d), lambda i: (i, 0)),
            pl.BlockSpec((1, d), lambda i: (0, 0)),
            pl.BlockSpec((d, 2 * A_WIDTH), lambda i: (0, 0)),
            pl.BlockSpec((1, A_WIDTH), lambda i: (0, 0)),
            pl.BlockSpec((1, A_WIDTH), lambda i: (0, 0)),
            pl.BlockSpec((1, A_GROUPS, CHUNK, CHUNK), sel),
            pl.BlockSpec((1, A_GROUPS, CHUNK, CHUNK), sel),
        ],
        out_specs=(pl.BlockSpec((tm, A_WIDTH), lambda i: (i, 0)),
                   pl.BlockSpec((tm, A_WIDTH), lambda i: (i, 0))),
        compiler_params=_params("parallel"),
        name="gmlp",
    )(x, g, w_a, ln_g, ln_b, ws_all, bs_all)


def _dot_sel_right(x, sel):
    hi, lo = _split2(x)
    return _dot(hi, sel) + _dot(lo, sel)


def _head_sum(x, ones_blk):
    cols = [_dot_sel_right(x[:, p * LANES:(p + 1) * LANES], ones_blk)
            for p in range(x.shape[1] // LANES)]
    return jnp.concatenate(cols, axis=1)


def _softplus(y):
    return jnp.maximum(y, 0.0) + jnp.log1p(jnp.exp(-jnp.abs(y)))


def _rwkv_prep_kernel(z_ref, prev_ref, mu_ref, w0_ref, a0_ref, kk_ref, ka_ref, rk_ref, wl_ref,
                      r_ref, kkn_ref, b_ref, km_ref, v_ref, ld_ref, g_ref, bonus_ref, *,
                      seq_len):
    tc = z_ref.shape[0]
    z = z_ref[...]
    row = lax.broadcasted_iota(jnp.int32, z.shape, 0)
    rolled = pltpu.roll(z, 1, 0)
    if seq_len >= tc:
        tiles_per_seq = seq_len // tc
        at_start = (pl.program_id(0) % tiles_per_seq) == 0
        last = prev_ref[SUBLANES - 1:SUBLANES, :]
        first = jnp.where(at_start, jnp.zeros_like(last), last)
        prev = jnp.where(row == 0, first, rolled)
    else:
        prev = jnp.where(row % seq_len == 0, prev_ref[...], rolled)
    zs = z + (prev - z) * mu_ref[...]

    r = zs[:, :B_WIDTH]
    k = zs[:, B_WIDTH:2 * B_WIDTH]
    v = zs[:, 2 * B_WIDTH:3 * B_WIDTH]
    lo0 = zs[:, 3 * B_WIDTH:3 * B_WIDTH + LANES]
    lane = lax.broadcasted_iota(jnp.int32, lo0.shape, 1)
    lo0 = jnp.where(lane < 64, jnp.tanh(lo0), lo0)
    lo12 = jax.nn.sigmoid(zs[:, 3 * B_WIDTH + LANES:])
    lora_in = jnp.concatenate([lo0, lo12], axis=1).astype(BF16)
    lora = _dot(lora_in, wl_ref[...])
    lw = lora[:, :B_WIDTH]
    la = lora[:, B_WIDTH:2 * B_WIDTH]
    g = lora[:, 2 * B_WIDTH:]

    w_raw = -_softplus(-(w0_ref[...] + lw)) - 0.5
    ld = -jnp.exp(w_raw)
    a = jax.nn.sigmoid(a0_ref[...] + la)

    rl = lax.broadcasted_iota(jnp.int32, (LANES, LANES), 0)
    cl = lax.broadcasted_iota(jnp.int32, (LANES, LANES), 1)
    ones_blk = jnp.where((rl // B_HEAD) == (cl // B_HEAD), 1.0, 0.0).astype(BF16)

    kk = k * kk_ref[...]
    nrm = jnp.maximum(jnp.sqrt(_head_sum(kk * kk, ones_blk)), 1e-12)
    kk = kk / nrm
    k_mod = k * (1.0 + (a - 1.0) * ka_ref[...])
    bonus = _head_sum(r * k_mod * rk_ref[...], ones_blk) * v

    r_ref[...] = r
    kkn_ref[...] = kk
    b_ref[...] = kk * a
    km_ref[...] = k_mod
    v_ref[...] = v
    ld_ref[...] = ld
    g_ref[...] = g
    bonus_ref[...] = bonus


def _rwkv_prep(zb, prev, row0, nrows, seq_len, tc, p):
    t0 = row0 // tc
    if seq_len >= tc:
        per8 = tc // SUBLANES
        prev_spec = pl.BlockSpec(
            (SUBLANES, B_COLS_PAD), lambda i: (jnp.maximum((i + t0) * per8 - 1, 0), 0))
        prev_arr = zb
    else:
        prev_spec = pl.BlockSpec((tc, B_COLS_PAD), lambda i: (i, 0))
        prev_arr = prev
    vec = lambda width: pl.BlockSpec((1, width), lambda i: (0, 0))
    out = jax.ShapeDtypeStruct((nrows, B_WIDTH), F32)
    return pl.pallas_call(
        functools.partial(_rwkv_prep_kernel, seq_len=seq_len),
        out_shape=(out,) * 8,
        grid=(nrows // tc,),
        in_specs=[
            pl.BlockSpec((tc, B_COLS_PAD), lambda i: (i + t0, 0)),
            prev_spec,
            vec(B_COLS_PAD), vec(B_WIDTH), vec(B_WIDTH), vec(B_WIDTH), vec(B_WIDTH), vec(B_WIDTH),
            pl.BlockSpec((B_LORA_PAD, 3 * B_WIDTH), lambda i: (0, 0)),
        ],
        out_specs=(pl.BlockSpec((tc, B_WIDTH), lambda i: (i, 0)),) * 8,
        compiler_params=_params("parallel"),
        name="rwkv_prep",
    )(zb, prev_arr, p["mu_b"], p["w0"], p["a0"], p["k_k"], p["k_a"], p["r_k"], p["w_lora"])


def _rwkv_chunk_kernel(r_ref, kk_ref, b_ref, km_ref, v_ref, ld_ref, g_ref, bonus_ref,
                       gng_ref, gnb_ref, s0_ref, act_ref, sout_ref, s_sc, *, nseq):
    c = pl.program_id(2)
    C = RWKV_CHUNK
    tq = C // nseq
    rounds = int(math.log2(tq)) + 1

    @pl.when(c == 0)
    def _():
        rr = lax.broadcasted_iota(jnp.int32, (LANES, LANES), 0)
        cc = lax.broadcasted_iota(jnp.int32, (LANES, LANES), 1)
        for s in range(nseq):
            s_sc[s] = jnp.where((rr // B_HEAD) == (cc // B_HEAD), s0_ref[s, 0], 0.0)

    row = lax.broadcasted_iota(jnp.int32, (2 * C, 2 * C), 0)
    col = lax.broadcasted_iota(jnp.int32, (2 * C, 2 * C), 1)
    t_i = row % C
    j_i = col % C
    mask_n = ((t_i // tq) == (j_i // tq)) & (j_i < t_i) & (row < C)
    blockdiag = (row // B_HEAD) == (col // B_HEAD)
    head0 = lax.broadcasted_iota(jnp.int32, (2 * C, LANES), 1) < B_HEAD
    t_h = lax.broadcasted_iota(jnp.int32, (C, 2 * C), 0)
    col_h = lax.broadcasted_iota(jnp.int32, (C, 2 * C), 1)
    j_h = col_h % C
    same_h = (t_h // tq) == (j_h // tq)
    mask_r = same_h & (j_h <= t_h)
    head0_h = lax.broadcasted_iota(jnp.int32, (C, LANES), 1) < B_HEAD

    ld = ld_ref[...]
    ld_pad = jnp.concatenate([ld, jnp.zeros_like(ld)], axis=0)
    tri = jnp.where(mask_r & (col_h < C), 1.0, 0.0)
    allseq = jnp.where(same_h & (col_h < C), 1.0, 0.0)
    cum = _dot_sel(tri, ld_pad)
    cum_end = _dot_sel(allseq, ld_pad)
    p_t = jnp.exp(cum)
    p_prev = jnp.exp(cum - ld)
    p_inv = jnp.exp(-cum)
    p_tail = jnp.exp(cum_end - cum)
    p_end = jnp.exp(cum_end)

    kk = kk_ref[...]
    b = b_ref[...]
    km = km_ref[...]
    vv = v_ref[...]
    am = -kk * p_prev
    rm = r_ref[...] * p_t
    x = jnp.concatenate([am, rm], axis=0)
    y = jnp.concatenate([b * p_inv, km * p_inv], axis=0)
    zt = jnp.concatenate([b * p_tail, km * p_tail], axis=0)

    xs_parts, rs_parts = [], []
    for s in range(nseq):
        rows = slice(s * tq, (s + 1) * tq)
        xs = jnp.concatenate([am[rows], rm[rows]], axis=0)
        xs0 = _dot3(xs, s_sc[s], dot=_dot_nt)
        xs_parts.append(xs0[:tq])
        rs_parts.append(xs0[tq:])
    x0 = xs_parts[0] if nseq == 1 else jnp.concatenate(xs_parts, axis=0)
    r0 = rs_parts[0] if nseq == 1 else jnp.concatenate(rs_parts, axis=0)

    w_heads, o_heads = [], []
    for hd in range(2):
        xm = jnp.where(head0 if hd == 0 else ~head0, x, 0.0)
        gm = _dot3(xm, y, dot=_dot_nt)
        nmat = jnp.where(mask_n, gm, 0.0)
        mr = jnp.where(mask_r, gm[C:], 0.0)
        w = jnp.concatenate([x0, vv], axis=0)
        for it in range(rounds):
            w = w + _dot3(nmat, w)
            if it + 1 < rounds:
                nmat = _dot3(nmat, nmat)
        w_heads.append(w)
        o_heads.append(_dot3(mr, w))
    w = jnp.where(head0, w_heads[0], w_heads[1])
    o = r0 + jnp.where(head0_h, o_heads[0], o_heads[1])

    for s in range(nseq):
        rows = slice(s * tq, (s + 1) * tq)
        ws = jnp.concatenate([w[:C][rows], vv[rows]], axis=0)
        zs = jnp.concatenate([zt[:C][rows], zt[C:][rows]], axis=0)
        if 2 * tq < LANES:
            pad = jnp.zeros((LANES - 2 * tq, LANES), F32)
            ws = jnp.concatenate([ws, pad], axis=0)
            zs = jnp.concatenate([zs, pad], axis=0)
        upd = _dot3(ws.T, zs)
        pe = p_end[s * tq:s * tq + 1, :]
        s_sc[s] = s_sc[s] * pe + jnp.where(blockdiag, upd, 0.0)

    @pl.when(c == pl.num_programs(2) - 1)
    def _():
        for s in range(nseq):
            sout_ref[s, 0] = s_sc[s]

    ones_blk = jnp.where(blockdiag, 1.0, 0.0).astype(BF16)
    mean = _dot_sel_right(o, ones_blk) * (1.0 / B_HEAD)
    oc = o - mean
    var = _dot_sel_right(oc * oc, ones_blk) * (1.0 / B_HEAD)
    on = oc * lax.rsqrt(var + RWKV_GN_EPS) * gng_ref[...] + gnb_ref[...] + bonus_ref[...]
    act_ref[...] = (on * g_ref[...]).astype(BF16)


def _rwkv_chunk(prep, gn_g, gn_b, s0, nseq, seq_len):
    r, kk, b, km, v, ld, g, bonus = prep
    rows = r.shape[0]
    C = RWKV_CHUNK
    n_groups = rows // (nseq * seq_len)
    n_chunks = nseq * seq_len // C
    tok = pl.BlockSpec((C, LANES), lambda gi, p, c: (gi * n_chunks + c, p))
    vec = pl.BlockSpec((1, LANES), lambda gi, p, c: (0, p))
    st = pl.BlockSpec((nseq, 1, LANES, LANES), lambda gi, p, c: (gi, p, 0, 0))
    return pl.pallas_call(
        functools.partial(_rwkv_chunk_kernel, nseq=nseq),
        out_shape=(jax.ShapeDtypeStruct((rows, B_WIDTH), BF16),
                   jax.ShapeDtypeStruct(s0.shape, F32)),
        grid=(n_groups, B_PAIRS, n_chunks),
        in_specs=[tok] * 8 + [vec, vec, st],
        out_specs=(tok, st),
        scratch_shapes=[pltpu.VMEM((nseq, LANES, LANES), F32)],
        compiler_params=_params("parallel", "parallel", "arbitrary"),
        name="rwkv_chunk",
    )(r, kk, b, km, v, ld, g, bonus, gn_g, gn_b, s0)


def _conv_kernel(z_ref, buf_ref, dw_ref, dwb_ref, lng_ref, lnb_ref, act_ref, tail_ref, ext_sc):
    tc = z_ref.shape[0]
    H = CONV_HALO

    @pl.when(pl.program_id(1) == 0)
    def _():
        ext_sc[0:H, :] = buf_ref[0]

    z = z_ref[...]
    ext_sc[H:H + tc, :] = z[:, :C_WIDTH] * jax.nn.sigmoid(z[:, C_WIDTH:])
    off = H - (CONV_W - 1)
    acc = jnp.zeros((tc, C_WIDTH), F32)
    for j in range(CONV_W):
        acc = acc + dw_ref[j:j + 1, :] * ext_sc[off + j:off + j + tc, :]
    y = _layernorm(acc + dwb_ref[...], lng_ref[...], lnb_ref[...])
    act_ref[...] = (y * jax.nn.sigmoid(y)).astype(BF16)
    tail = ext_sc[tc:tc + H, :]
    tail_ref[0] = tail
    ext_sc[0:H, :] = tail


def _conv(zc, buf, dw, dw_b, ln_g, ln_b, row0, n_seqs, seq_len, tc):
    tiles = seq_len // tc
    t0 = row0 // tc
    vec = pl.BlockSpec((1, C_WIDTH), lambda s, i: (0, 0))
    return pl.pallas_call(
        _conv_kernel,
        out_shape=(jax.ShapeDtypeStruct((n_seqs * seq_len, C_WIDTH), BF16),
                   jax.ShapeDtypeStruct((n_seqs, CONV_HALO, C_WIDTH), F32)),
        grid=(n_seqs, tiles),
        in_specs=[
            pl.BlockSpec((tc, 2 * C_WIDTH), lambda s, i: (t0 + s * tiles + i, 0)),
            pl.BlockSpec((1, CONV_HALO, C_WIDTH), lambda s, i: (s, 0, 0)),
            pl.BlockSpec((CONV_HALO, C_WIDTH), lambda s, i: (0, 0)),
            vec, vec, vec,
        ],
        out_specs=(pl.BlockSpec((tc, C_WIDTH), lambda s, i: (s * tiles + i, 0)),
                   pl.BlockSpec((1, CONV_HALO, C_WIDTH), lambda s, i: (s, 0, 0))),
        scratch_shapes=[pltpu.VMEM((CONV_HALO + tc, C_WIDTH), F32)],
        compiler_params=_params("parallel", "arbitrary"),
        name="conv",
    )(zc, buf, dw, dw_b, ln_g, ln_b)


def _merge_kernel(x_ref, g_ref, wg0_ref, wg1_ref, wg2_ref, aa_ref, ab_ref, ac_ref,
                  woa_ref, wob_ref, woc_ref, wout_ref, o_ref, h_sc, acc_sc):
    j = pl.program_id(1)

    @pl.when(j == 0)
    def _():
        h_sc[...] = _rms(x_ref[...], g_ref[...]).astype(BF16)
        acc_sc[...] = jnp.zeros_like(acc_sc)

    h = h_sc[...]
    merged = jax.nn.sigmoid(_dot(h, wg0_ref[...])) * _dot(aa_ref[...], woa_ref[...])
    merged += jax.nn.sigmoid(_dot(h, wg1_ref[...])) * _dot(ab_ref[...], wob_ref[...])
    merged += jax.nn.sigmoid(_dot(h, wg2_ref[...])) * _dot(ac_ref[...], woc_ref[...])
    acc_sc[...] += _dot(merged.astype(BF16), wout_ref[...])

    @pl.when(j == pl.num_programs(1) - 1)
    def _():
        o_ref[...] = x_ref[...] + acc_sc[...]


def _merge(x, g, w_g, act_a, act_b, act_c, wo_a, wo_b, wo_c, w_out, tn=512):
    n, d = x.shape
    tm = ROW_TILE
    nj = d // tn
    row = lambda width: pl.BlockSpec((tm, width), lambda i, j: (i, 0))
    colw = lambda k: pl.BlockSpec((k, tn), lambda i, j: (0, j))
    return pl.pallas_call(
        _merge_kernel,
        out_shape=jax.ShapeDtypeStruct((n, d), F32),
        grid=(n // tm, nj),
        in_specs=[
            row(d),
            pl.BlockSpec((1, d), lambda i, j: (0, 0)),
            pl.BlockSpec((d, tn), lambda i, j: (0, j)),
            pl.BlockSpec((d, tn), lambda i, j: (0, nj + j)),
            pl.BlockSpec((d, tn), lambda i, j: (0, 2 * nj + j)),
            row(A_WIDTH), row(B_WIDTH), row(C_WIDTH),
            colw(A_WIDTH), colw(B_WIDTH), colw(C_WIDTH),
            pl.BlockSpec((tn, d), lambda i, j: (j, 0)),
        ],
        out_specs=row(d),
        scratch_shapes=[pltpu.VMEM((tm, d), BF16), pltpu.VMEM((tm, d), F32)],
        compiler_params=_params("parallel", "arbitrary"),
        name="merge",
    )(x, g, w_g, w_g, w_g, act_a, act_b, act_c, wo_a, wo_b, wo_c, w_out)


def _xattn_kernel(x_ref, g_ref, wq_ref, k_ref, v_ref, wo_ref, o_ref, *, nseq):
    x = x_ref[...]
    tq = x.shape[0]
    rows_per_seq = tq // nseq
    h = _rms(x, g_ref[...]).astype(BF16)
    q = _dot(h, wq_ref[...])
    scale = X_HEAD_DIM ** -0.5
    seq_out = []
    for s in range(nseq):
        rows = slice(s * rows_per_seq, (s + 1) * rows_per_seq)
        head_out = []
        for hd in range(X_HEADS):
            cols = slice(hd * X_HEAD_DIM, (hd + 1) * X_HEAD_DIM)
            kh = k_ref[s, :, cols].astype(BF16)
            vh = v_ref[s, :, cols].astype(BF16)
            sc = _dot_nt(q[rows, cols].astype(BF16), kh) * scale
            e = jnp.exp(sc - jnp.max(sc, axis=-1, keepdims=True))
            pr = e / jnp.sum(e, axis=-1, keepdims=True)
            head_out.append(_dot(pr.astype(BF16), vh))
        seq_out.append(jnp.concatenate(head_out, axis=1))
    o = seq_out[0] if nseq == 1 else jnp.concatenate(seq_out, axis=0)
    o_ref[...] = x + _dot(o.astype(BF16), wo_ref[...])


def _xattn(x, g, w_q, mem_k, mem_v, w_o, row0, nrows, seq_len, tq):
    d = x.shape[1]
    t0 = row0 // tq
    if seq_len >= tq:
        nseq = 1
        per_seq = seq_len // tq
        kv_idx = lambda i: (i // per_seq, 0, 0)
    else:
        nseq = tq // seq_len
        kv_idx = lambda i: (i, 0, 0)
    kv = pl.BlockSpec((nseq, N_MEM, X_WIDTH), kv_idx)
    return pl.pallas_call(
        functools.partial(_xattn_kernel, nseq=nseq),
        out_shape=jax.ShapeDtypeStruct((nrows, d), F32),
        grid=(nrows // tq,),
        in_specs=[
            pl.BlockSpec((tq, d), lambda i: (i + t0, 0)),
            pl.BlockSpec((1, d), lambda i: (0, 0)),
            pl.BlockSpec((d, X_WIDTH), lambda i: (0, 0)),
            kv, kv,
            pl.BlockSpec((X_WIDTH, d), lambda i: (0, 0)),
        ],
        out_specs=pl.BlockSpec((tq, d), lambda i: (i, 0)),
        compiler_params=_params("parallel"),
        name="xattn",
    )(x, g, w_q, mem_k, mem_v, w_o)


def _final_norm_kernel(x_ref, g_ref, o_ref):
    o_ref[...] = _rms(x_ref[...], g_ref[...])


def _final_norm(x, g):
    n, d = x.shape
    tm = ROW_TILE
    return pl.pallas_call(
        _final_norm_kernel,
        out_shape=jax.ShapeDtypeStruct((n, d), F32),
        grid=(n // tm,),
        in_specs=[pl.BlockSpec((tm, d), lambda i: (i, 0)), pl.BlockSpec((1, d), lambda i: (0, 0))],
        out_specs=pl.BlockSpec((tm, d), lambda i: (i, 0)),
        compiler_params=_params("parallel"),
        name="final_norm",
    )(x, g)


def _pair_states(s):
    n = s.shape[0]
    s = s.reshape(n, B_PAIRS, 2 * B_HEAD, B_HEAD)
    return jnp.concatenate([s, s], axis=-1)


def _unpair_states(s):
    n = s.shape[0]
    top = s[:, :, :B_HEAD, :B_HEAD]
    bot = s[:, :, B_HEAD:, B_HEAD:]
    return jnp.stack([top, bot], axis=2).reshape(n, B_HEADS, B_HEAD, B_HEAD)


def _layer_params(l, w):
    bf = lambda a: a.astype(BF16)
    row = lambda a: a.reshape(1, -1)
    a_cols = 2 * A_WIDTH
    c0 = a_cols + B_COLS
    g0 = c0 + 2 * C_WIDTH
    w_in = w["w_in"][l]
    p = {}
    for name in ("n_ffn1", "n_mix", "n_x", "n_mem", "n_ffn2", "w0", "a0", "k_k", "k_a", "r_k",
                 "gn_g", "gn_b", "ln_a_g", "ln_a_b", "dw_b", "ln_c_g", "ln_c_b"):
        p[name] = row(w[name][l])
    for name in ("ffn1_w1", "ffn1_w3", "ffn1_w2", "ffn2_w1", "ffn2_w3", "ffn2_w2", "wo_a", "wo_b",
                 "wo_c", "w_out", "w_xq", "w_xo"):
        p[name] = bf(w[name][l])
    p["w_a"] = bf(w_in[:, :a_cols])
    p["w_b"] = bf(jnp.pad(w_in[:, a_cols:c0], ((0, 0), (0, B_COLS_PAD - B_COLS))))
    p["w_c"] = bf(w_in[:, c0:g0])
    p["w_g"] = bf(w_in[:, g0:])
    p["w_kv"] = bf(jnp.concatenate([w["w_xk"][l], w["w_xv"][l]], axis=1))
    p["mu_b"] = row(jnp.pad(w["mu_b"][l], (0, B_COLS_PAD - B_COLS)))
    wl = jnp.zeros((B_LORA_PAD, 3 * B_WIDTH), F32)
    wl = wl.at[0:64, 0:B_WIDTH].set(w["w2_decay"][l])
    wl = wl.at[64:128, B_WIDTH:2 * B_WIDTH].set(w["a2_aaa"][l])
    wl = wl.at[128:B_LORA, 2 * B_WIDTH:].set(w["g2_gate"][l])
    p["w_lora"] = bf(wl)
    p["dw"] = jnp.pad(w["dw"][l], ((0, CONV_HALO - CONV_W), (0, 0)))
    return p


def kernel(x_prompt, x_sample, mem_prompt, state_rwkv, state_shift, state_conv, cache_mem_k, cache_mem_v, n_ffn1, ffn1_w1, ffn1_w3, ffn1_w2, n_mix, w_in, mu_b, w0, w2_decay, a0, a2_aaa, g2_gate, k_k, k_a, r_k, gn_g, gn_b, wo_b, ln_a_g, ln_a_b, w_s, b_s, wo_a, dw, dw_b, ln_c_g, ln_c_b, wo_c, w_out, n_x, n_mem, w_xq, w_xk, w_xv, w_xo, n_ffn2, ffn2_w1, ffn2_w3, ffn2_w2, n_final):
    weights = dict(
        n_ffn1=n_ffn1, ffn1_w1=ffn1_w1, ffn1_w3=ffn1_w3, ffn1_w2=ffn1_w2, n_mix=n_mix, w_in=w_in,
        mu_b=mu_b, w0=w0, w2_decay=w2_decay, a0=a0, a2_aaa=a2_aaa, g2_gate=g2_gate, k_k=k_k,
        k_a=k_a, r_k=r_k, gn_g=gn_g, gn_b=gn_b, wo_b=wo_b, ln_a_g=ln_a_g, ln_a_b=ln_a_b,
        wo_a=wo_a, dw=dw, dw_b=dw_b, ln_c_g=ln_c_g, ln_c_b=ln_c_b, wo_c=wo_c,
        w_out=w_out, n_x=n_x, n_mem=n_mem, w_xq=w_xq, w_xk=w_xk, w_xv=w_xv, w_xo=w_xo,
        n_ffn2=n_ffn2, ffn2_w1=ffn2_w1, ffn2_w3=ffn2_w3, ffn2_w2=ffn2_w2)
    depth = w_in.shape[0]
    bp, tp, d = x_prompt.shape
    bs, ts, _ = x_sample.shape
    n_p, n_s = bp * tp, bs * ts
    x = jnp.concatenate([x_prompt.reshape(n_p, d), x_sample.reshape(n_s, d)], axis=0)
    mem = mem_prompt.reshape(bp * N_MEM, d)
    reps = CHUNK // ts

    outs = {k: [] for k in ("rwkv_p", "shift_p", "conv_p", "mk_p", "mv_p",
                            "rwkv_s", "shift_s", "conv_s", "gv_s")}
    for l in range(depth):
        p = _layer_params(l, weights)
        ws_all = jnp.stack([w_s[l], jnp.tile(w_s[l][:, :ts, :ts], (1, reps, reps))])
        bias_s = jnp.tile(b_s[l][:, :ts], (1, reps))
        bs_all = jnp.stack([jnp.broadcast_to(b_s[l][:, :, None], (A_GROUPS, CHUNK, CHUNK)),
                            jnp.broadcast_to(bias_s[:, :, None], (A_GROUPS, CHUNK, CHUNK))])

        kv = _norm_mm(mem, p["n_mem"], p["w_kv"], tn=2 * X_WIDTH)
        mk = kv[:, :X_WIDTH].reshape(bp, N_MEM, X_WIDTH)
        mv = kv[:, X_WIDTH:].reshape(bp, N_MEM, X_WIDTH)

        x = _ffn(x, p["n_ffn1"], p["ffn1_w1"], p["ffn1_w3"], p["ffn1_w2"])

        act_a, v_n = _gmlp(x, p["n_mix"], p["w_a"], p["ln_a_g"], p["ln_a_b"], ws_all, bs_all, n_p, ts)
        zb = _norm_mm(x, p["n_mix"], p["w_b"], tn=B_COLS_PAD // 3)
        zc = _norm_mm(x, p["n_mix"], p["w_c"], tn=2 * C_WIDTH)

        shift_exp = jnp.pad(state_shift[l][:, None, :],
                            ((0, 0), (0, ts - 1), (0, B_COLS_PAD - B_COLS))).reshape(n_s, B_COLS_PAD)
        prep_p = _rwkv_prep(zb, None, 0, n_p, tp, 256, p)
        prep_s = _rwkv_prep(zb, shift_exp, n_p, n_s, ts, 256, p)
        zero_state = jnp.zeros((bp, B_PAIRS, LANES, LANES), F32)
        act_bp, s_p = _rwkv_chunk(prep_p, p["gn_g"], p["gn_b"], zero_state, 1, tp)
        act_bs, s_s = _rwkv_chunk(prep_s, p["gn_g"], p["gn_b"], _pair_states(state_rwkv[l]),
                                  RWKV_CHUNK // ts, ts)
        act_b = jnp.concatenate([act_bp, act_bs], axis=0)

        pad_buf = lambda b: jnp.pad(b, ((0, 0), (CONV_HALO - (CONV_W - 1), 0), (0, 0)))
        act_cp, tail_p = _conv(zc, jnp.zeros((bp, CONV_HALO, C_WIDTH), F32), p["dw"], p["dw_b"],
                               p["ln_c_g"], p["ln_c_b"], 0, bp, tp, 256)
        act_cs, tail_s = _conv(zc, pad_buf(state_conv[l]), p["dw"], p["dw_b"],
                               p["ln_c_g"], p["ln_c_b"], n_p, bs, ts, ts)
        act_c = jnp.concatenate([act_cp, act_cs], axis=0)

        x = _merge(x, p["n_mix"], p["w_g"], act_a, act_b, act_c,
                   p["wo_a"], p["wo_b"], p["wo_c"], p["w_out"])

        xp = _xattn(x, p["n_x"], p["w_xq"], mk, mv, p["w_xo"], 0, n_p, tp, 256)
        xs = _xattn(x, p["n_x"], p["w_xq"], cache_mem_k[l].reshape(bs, N_MEM, X_WIDTH),
                    cache_mem_v[l].reshape(bs, N_MEM, X_WIDTH), p["w_xo"], n_p, n_s, ts, 64)
        x = jnp.concatenate([xp, xs], axis=0)

        x = _ffn(x, p["n_ffn2"], p["ffn2_w1"], p["ffn2_w3"], p["ffn2_w2"])

        zb_p = zb[:n_p].reshape(bp, tp, B_COLS_PAD)
        zb_s = zb[n_p:].reshape(bs, ts, B_COLS_PAD)
        outs["rwkv_p"].append(_unpair_states(s_p))
        outs["shift_p"].append(zb_p[:, -1, :B_COLS])
        outs["conv_p"].append(tail_p[:, CONV_HALO - (CONV_W - 1):])
        outs["mk_p"].append(mk.reshape(bp, N_MEM, X_HEADS, X_HEAD_DIM))
        outs["mv_p"].append(mv.reshape(bp, N_MEM, X_HEADS, X_HEAD_DIM))
        outs["rwkv_s"].append(_unpair_states(s_s))
        outs["shift_s"].append(zb_s[:, -1, :B_COLS])
        outs["conv_s"].append(tail_s[:, CONV_HALO - (CONV_W - 1):])
        outs["gv_s"].append(v_n[n_p:].reshape(bs, ts, A_WIDTH))

    y = _final_norm(x, n_final.reshape(1, d))
    st = lambda k: jnp.stack(outs[k])
    return (y[:n_p].reshape(bp, tp, d), y[n_p:].reshape(bs, ts, d),
            st("rwkv_p"), st("shift_p"), st("conv_p"), st("mk_p"), st("mv_p"),
            st("rwkv_s"), st("shift_s"), st("conv_s"), st("gv_s"))
```

```python
import functools
import math

import jax
import jax.numpy as jnp
from jax import lax
from jax.experimental import pallas as pl
from jax.experimental.pallas import tpu as pltpu

F32 = jnp.float32
BF16 = jnp.bfloat16

LANES = 128
SUBLANES = 8
VMEM_LIMIT_BYTES = 56 * 1024 * 1024

D_MODEL = 2048
A_WIDTH = 512
A_GROUPS = 4
CHUNK = 128
B_WIDTH = 1024
B_HEAD = 64
B_HEADS = 16
B_PAIRS = B_HEADS // 2
B_LORA = 64 + 64 + 160
B_LORA_PAD = 384
B_COLS = 3 * B_WIDTH + B_LORA
B_COLS_PAD = 3 * B_WIDTH + B_LORA_PAD
RWKV_GN_EPS = 64e-5
RWKV_CHUNK = 64
C_WIDTH = 512
CONV_W = 31
CONV_HALO = 32
N_MEM = 256
X_HEADS = 4
X_HEAD_DIM = 128
X_WIDTH = 512
D_FF = 5632
EPS = 1e-6
LN_EPS = 1e-5

ROW_TILE = 512


def _params(*sem):
    return pltpu.CompilerParams(dimension_semantics=sem, vmem_limit_bytes=VMEM_LIMIT_BYTES)


def _into_args(into):
    if into is None:
        return {"specs": [], "args": []}
    return {"specs": [pl.BlockSpec(memory_space=pl.ANY)], "args": [into]}


def _ignore_last_input(kernel_fn, n_inputs):
    def wrapped(*refs):
        return kernel_fn(*refs[:n_inputs], *refs[n_inputs + 1:])
    return wrapped


def _rms(x, g):
    return x * lax.rsqrt(jnp.mean(x * x, axis=-1, keepdims=True) + EPS) * g


def _layernorm(x, g, b):
    xc = x - jnp.mean(x, axis=-1, keepdims=True)
    return xc * lax.rsqrt(jnp.mean(xc * xc, axis=-1, keepdims=True) + LN_EPS) * g + b


def _dot(a, b):
    return jnp.dot(a, b, preferred_element_type=F32)


def _dot_nt(a, b):
    return lax.dot_general(a, b, (((1,), (1,)), ((), ())), preferred_element_type=F32)


def _split2(x):
    hi = x.astype(BF16)
    lo = (x - hi.astype(F32)).astype(BF16)
    return hi, lo


def _split3(x):
    hi = x.astype(BF16)
    r1 = x - hi.astype(F32)
    mid = r1.astype(BF16)
    lo = (r1 - mid.astype(F32)).astype(BF16)
    return hi, mid, lo


SCAN_MXU_PASSES = 3


def _split(x):
    return _split2(x) if SCAN_MXU_PASSES == 3 else (x.astype(BF16),)


def _dotp(a_parts, b_parts, dot=_dot):
    acc = dot(a_parts[0], b_parts[0])
    if len(a_parts) > 1:
        acc = acc + (dot(a_parts[0], b_parts[1]) + dot(a_parts[1], b_parts[0]))
    return acc


def _dot_sel(sel, x, terms=3):
    sel = sel.astype(BF16)
    parts = _split3(x) if terms == 3 else _split2(x)
    acc = _dot(sel, parts[0])
    for p in parts[1:]:
        acc = acc + _dot(sel, p)
    return acc


def _ffn_kernel(x_ref, g_ref, w1_ref, w3_ref, w2_ref, o_ref, h_sc, acc_sc):
    j = pl.program_id(1)

    @pl.when(j == 0)
    def _():
        h_sc[...] = _rms(x_ref[...], g_ref[...]).astype(BF16)
        acc_sc[...] = jnp.zeros_like(acc_sc)

    h = h_sc[...]
    a = _dot(h, w1_ref[...])
    b = _dot(h, w3_ref[...])
    act = (a * jax.nn.sigmoid(a) * b).astype(BF16)
    acc_sc[...] += _dot(act, w2_ref[...])

    @pl.when(j == pl.num_programs(1) - 1)
    def _():
        o_ref[...] = x_ref[...] + 0.5 * acc_sc[...]


def _ffn(x, g, w1, w3, w2, tf=512):
    n, d = x.shape
    f = w1.shape[1]
    tm = ROW_TILE
    return pl.pallas_call(
        _ffn_kernel,
        out_shape=jax.ShapeDtypeStruct((n, d), F32),
        grid=(n // tm, f // tf),
        in_specs=[
            pl.BlockSpec((tm, d), lambda i, j: (i, 0)),
            pl.BlockSpec((1, d), lambda i, j: (0, 0)),
            pl.BlockSpec((d, tf), lambda i, j: (0, j)),
            pl.BlockSpec((d, tf), lambda i, j: (0, j)),
            pl.BlockSpec((tf, d), lambda i, j: (j, 0)),
        ],
        out_specs=pl.BlockSpec((tm, d), lambda i, j: (i, 0)),
        scratch_shapes=[pltpu.VMEM((tm, d), BF16), pltpu.VMEM((tm, d), F32)],
        compiler_params=_params("parallel", "arbitrary"),
        name="ffn",
    )(x, g, w1, w3, w2)


def _norm_mm_kernel(x_ref, g_ref, w_ref, o_ref, h_sc):
    @pl.when(pl.program_id(1) == 0)
    def _():
        h_sc[...] = _rms(x_ref[...], g_ref[...]).astype(BF16)

    o_ref[...] = _dot(h_sc[...], w_ref[...])


def _norm_mm(x, g, w, tn):
    n, d = x.shape
    cols = w.shape[1]
    tm = min(ROW_TILE, n)
    return pl.pallas_call(
        _norm_mm_kernel,
        out_shape=jax.ShapeDtypeStruct((n, cols), F32),
        grid=(n // tm, cols // tn),
        in_specs=[
            pl.BlockSpec((tm, d), lambda i, j: (i, 0)),
            pl.BlockSpec((1, d), lambda i, j: (0, 0)),
            pl.BlockSpec((d, tn), lambda i, j: (0, j)),
        ],
        out_specs=pl.BlockSpec((tm, tn), lambda i, j: (i, j)),
        scratch_shapes=[pltpu.VMEM((tm, d), BF16)],
        compiler_params=_params("parallel", "arbitrary"),
        name="norm_mm",
    )(x, g, w)


def _gmlp_kernel(x_ref, g_ref, w_ref, lng_ref, lnb_ref, ws_ref, bs_ref, act_ref, vn_ref, *,
                 n_prompt_tiles, short_len):
    is_sample = pl.program_id(0) >= n_prompt_tiles
    h = _rms(x_ref[...], g_ref[...]).astype(BF16)
    z = jax.nn.gelu(_dot(h, w_ref[...]))
    u = z[:, :A_WIDTH]
    v_n = _layernorm(z[:, A_WIDTH:], lng_ref[...], lnb_ref[...])
    vn_ref[...] = v_n

    row = lax.broadcasted_iota(jnp.int32, (CHUNK, CHUNK), 0)
    col = lax.broadcasted_iota(jnp.int32, (CHUNK, CHUNK), 1)
    shift = jnp.where(is_sample, int(math.log2(short_len)), int(math.log2(CHUNK)))
    shift = jnp.broadcast_to(shift.astype(jnp.int32), (CHUNK, CHUNK))
    same_seq = lax.shift_right_logical(row, shift) == lax.shift_right_logical(col, shift)
    keep = (col <= row) & same_seq
    v_b = v_n.astype(BF16)
    tm = x_ref.shape[0]
    gw = A_WIDTH // A_GROUPS
    for grp in range(A_GROUPS):
        ws = jnp.where(keep, ws_ref[0, grp], 0.0).astype(BF16)
        bias = bs_ref[0, grp]
        for c in range(tm // CHUNK):
            rows = slice(c * CHUNK, (c + 1) * CHUNK)
            cols = slice(grp * gw, (grp + 1) * gw)
            mixed = _dot(ws, v_b[rows, cols]) + bias
            act_ref[rows, cols] = (u[rows, cols] * mixed).astype(BF16)


def _gmlp(x, g, w_a, ln_g, ln_b, ws_all, bs_all, n_prompt, short_len):
    n, d = x.shape
    tm = ROW_TILE
    npt = n_prompt // tm
    kern = functools.partial(_gmlp_kernel, n_prompt_tiles=npt, short_len=short_len)
    sel = lambda i: (jnp.where(i >= npt, 1, 0), 0, 0, 0)
    return pl.pallas_call(
        kern,
        out_shape=(jax.ShapeDtypeStruct((n, A_WIDTH), BF16), jax.ShapeDtypeStruct((n, A_WIDTH), F32)),
        grid=(n // tm,),
        in_specs=[
            pl.BlockSpec((tm, d), lambda i: (i, 0)),
            pl.BlockSpec((1, d), lambda i: (0, 0)),
            pl.BlockSpec((d, 2 * A_WIDTH), lambda i: (0, 0)),
            pl.BlockSpec((1, A_WIDTH), lambda i: (0, 0)),
            pl.BlockSpec((1, A_WIDTH), lambda i: (0, 0)),
            pl.BlockSpec((1, A_GROUPS, CHUNK, CHUNK), sel),
            pl.BlockSpec((1, A_GROUPS, CHUNK, CHUNK), sel),
        ],
        out_specs=(pl.BlockSpec((tm, A_WIDTH), lambda i: (i, 0)),
                   pl.BlockSpec((tm, A_WIDTH), lambda i: (i, 0))),
        compiler_params=_params("parallel"),
        name="gmlp",
    )(x, g, w_a, ln_g, ln_b, ws_all, bs_all)


def _dot_sel_right(x, sel):
    hi, lo = _split2(x)
    return _dot(hi, sel) + _dot(lo, sel)


def _head_sum(x, ones_blk):
    cols = [_dot_sel_right(x[:, p * LANES:(p + 1) * LANES], ones_blk)
            for p in range(x.shape[1] // LANES)]
    return jnp.concatenate(cols, axis=1)


def _softplus(y):
    return jnp.maximum(y, 0.0) + jnp.log1p(jnp.exp(-jnp.abs(y)))


def _rwkv_prep_kernel(z_ref, prev_ref, mu_ref, w0_ref, a0_ref, kk_ref, ka_ref, rk_ref, wl_ref,
                      r_ref, kkn_ref, b_ref, km_ref, v_ref, ld_ref, g_ref, bonus_ref, *,
                      seq_len):
    tc = z_ref.shape[0]
    z = z_ref[...]
    row = lax.broadcasted_iota(jnp.int32, z.shape, 0)
    rolled = pltpu.roll(z, 1, 0)
    if seq_len >= tc:
        tiles_per_seq = seq_len // tc
        at_start = (pl.program_id(0) % tiles_per_seq) == 0
        last = prev_ref[SUBLANES - 1:SUBLANES, :]
        first = jnp.where(at_start, jnp.zeros_like(last), last)
        prev = jnp.where(row == 0, first, rolled)
    else:
        prev = jnp.where(row % seq_len == 0, prev_ref[...], rolled)
    zs = z + (prev - z) * mu_ref[...]

    r = zs[:, :B_WIDTH]
    k = zs[:, B_WIDTH:2 * B_WIDTH]
    v = zs[:, 2 * B_WIDTH:3 * B_WIDTH]
    lo0 = zs[:, 3 * B_WIDTH:3 * B_WIDTH + LANES]
    lane = lax.broadcasted_iota(jnp.int32, lo0.shape, 1)
    lo0 = jnp.where(lane < 64, jnp.tanh(lo0), lo0)
    lo12 = jax.nn.sigmoid(zs[:, 3 * B_WIDTH + LANES:])
    lora_in = jnp.concatenate([lo0, lo12], axis=1).astype(BF16)
    lora = _dot(lora_in, wl_ref[...])
    lw = lora[:, :B_WIDTH]
    la = lora[:, B_WIDTH:2 * B_WIDTH]
    g = lora[:, 2 * B_WIDTH:]

    w_raw = -_softplus(-(w0_ref[...] + lw)) - 0.5
    ld = -jnp.exp(w_raw)
    a = jax.nn.sigmoid(a0_ref[...] + la)

    rl = lax.broadcasted_iota(jnp.int32, (LANES, LANES), 0)
    cl = lax.broadcasted_iota(jnp.int32, (LANES, LANES), 1)
    ones_blk = jnp.where((rl // B_HEAD) == (cl // B_HEAD), 1.0, 0.0).astype(BF16)

    kk = k * kk_ref[...]
    nrm = jnp.maximum(jnp.sqrt(_head_sum(kk * kk, ones_blk)), 1e-12)
    kk = kk / nrm
    k_mod = k * (1.0 + (a - 1.0) * ka_ref[...])
    bonus = _head_sum(r * k_mod * rk_ref[...], ones_blk) * v

    r_ref[...] = r
    kkn_ref[...] = kk
    b_ref[...] = kk * a
    km_ref[...] = k_mod
    v_ref[...] = v
    ld_ref[...] = ld
    g_ref[...] = g
    bonus_ref[...] = bonus


def _rwkv_prep(zb, prev, row0, nrows, seq_len, tc, p):
    t0 = row0 // tc
    if seq_len >= tc:
        per8 = tc // SUBLANES
        prev_spec = pl.BlockSpec(
            (SUBLANES, B_COLS_PAD), lambda i: (jnp.maximum((i + t0) * per8 - 1, 0), 0))
        prev_arr = zb
    else:
        prev_spec = pl.BlockSpec((tc, B_COLS_PAD), lambda i: (i, 0))
        prev_arr = prev
    vec = lambda width: pl.BlockSpec((1, width), lambda i: (0, 0))
    out = jax.ShapeDtypeStruct((nrows, B_WIDTH), F32)
    return pl.pallas_call(
        functools.partial(_rwkv_prep_kernel, seq_len=seq_len),
        out_shape=(out,) * 8,
        grid=(nrows // tc,),
        in_specs=[
            pl.BlockSpec((tc, B_COLS_PAD), lambda i: (i + t0, 0)),
            prev_spec,
            vec(B_COLS_PAD), vec(B_WIDTH), vec(B_WIDTH), vec(B_WIDTH), vec(B_WIDTH), vec(B_WIDTH),
            pl.BlockSpec((B_LORA_PAD, 3 * B_WIDTH), lambda i: (0, 0)),
        ],
        out_specs=(pl.BlockSpec((tc, B_WIDTH), lambda i: (i, 0)),) * 8,
        compiler_params=_params("parallel"),
        name="rwkv_prep",
    )(zb, prev_arr, p["mu_b"], p["w0"], p["a0"], p["k_k"], p["k_a"], p["r_k"], p["w_lora"])


def _rwkv_chunk_kernel(r_ref, kk_ref, b_ref, km_ref, v_ref, ld_ref, g_ref, bonus_ref,
                       gng_ref, gnb_ref, s0_ref, act_ref, sout_ref, s_sc, *, nseq):
    c = pl.program_id(1)
    C = RWKV_CHUNK
    tq = C // nseq
    rounds = int(math.log2(tq)) + 1
    pairs = range(B_PAIRS)
    lanes = lambda p: slice(p * LANES, (p + 1) * LANES)

    @pl.when(c == 0)
    def _():
        rr = lax.broadcasted_iota(jnp.int32, (LANES, LANES), 0)
        cc = lax.broadcasted_iota(jnp.int32, (LANES, LANES), 1)
        for s in range(nseq):
            for p in pairs:
                s_sc[s, p] = jnp.where((rr // B_HEAD) == (cc // B_HEAD), s0_ref[s, p], 0.0)

    row = lax.broadcasted_iota(jnp.int32, (2 * C, 2 * C), 0)
    col = lax.broadcasted_iota(jnp.int32, (2 * C, 2 * C), 1)
    t_i = row % C
    j_i = col % C
    mask_n = ((t_i // tq) == (j_i // tq)) & (j_i < t_i) & (row < C)
    blockdiag = (row // B_HEAD) == (col // B_HEAD)
    head0 = lax.broadcasted_iota(jnp.int32, (2 * C, LANES), 1) < B_HEAD
    t_h = lax.broadcasted_iota(jnp.int32, (C, 2 * C), 0)
    col_h = lax.broadcasted_iota(jnp.int32, (C, 2 * C), 1)
    j_h = col_h % C
    same_h = (t_h // tq) == (j_h // tq)
    mask_r = same_h & (j_h <= t_h)
    head0_h = lax.broadcasted_iota(jnp.int32, (C, LANES), 1) < B_HEAD

    ld = ld_ref[...]
    ld_pad = jnp.concatenate([ld, jnp.zeros_like(ld)], axis=0)
    tri = jnp.where(mask_r & (col_h < C), 1.0, 0.0)
    allseq = jnp.where(same_h & (col_h < C), 1.0, 0.0)
    cum = _dot_sel(tri, ld_pad)
    if nseq == 1:
        cum_end = jnp.broadcast_to(cum[C - 1:C, :], cum.shape)
    else:
        cum_end = _dot_sel(allseq, ld_pad)
    p_t = jnp.exp(cum)
    p_prev = jnp.exp(cum - ld)
    p_inv = jnp.exp(-cum)
    p_tail = jnp.exp(cum_end - cum)
    p_end = jnp.exp(cum_end)

    kk = kk_ref[...]
    b = b_ref[...]
    km = km_ref[...]
    vv = v_ref[...]
    am = -kk * p_prev
    rm = r_ref[...] * p_t
    x = jnp.concatenate([am, rm], axis=0)
    y = jnp.concatenate([b * p_inv, km * p_inv], axis=0)
    zt = jnp.concatenate([b * p_tail, km * p_tail], axis=0)


    x0, r0 = [], []
    for p in pairs:
        xs_parts, rs_parts = [], []
        for s in range(nseq):
            rows = slice(s * tq, (s + 1) * tq)
            xs = jnp.concatenate([am[rows, lanes(p)], rm[rows, lanes(p)]], axis=0)
            xs0 = _dotp(_split(xs), _split(s_sc[s, p]), dot=_dot_nt)
            xs_parts.append(xs0[:tq])
            rs_parts.append(xs0[tq:])
        x0.append(xs_parts[0] if nseq == 1 else jnp.concatenate(xs_parts, axis=0))
        r0.append(rs_parts[0] if nseq == 1 else jnp.concatenate(rs_parts, axis=0))

    chains = [(p, hd) for p in pairs for hd in range(2)]
    nmat, mr, w = {}, {}, {}
    for p in pairs:
        y_parts = _split(y[:, lanes(p)])
        for hd in range(2):
            xm = jnp.where(head0 if hd == 0 else ~head0, x[:, lanes(p)], 0.0)
            gm = _dotp(_split(xm), y_parts, dot=_dot_nt)
            nmat[p, hd] = jnp.where(mask_n, gm, 0.0)
            mr[p, hd] = jnp.where(mask_r, gm[C:], 0.0)
            w[p, hd] = jnp.concatenate([x0[p], vv[:, lanes(p)]], axis=0)

    for it in range(rounds):
        last = it + 1 == rounds
        for ch in chains:
            n_parts = _split(nmat[ch])
            w_parts = _split(w[ch])
            if last:
                w[ch] = w[ch] + _dotp(n_parts, w_parts)
            else:
                rhs = tuple(jnp.concatenate([a, bb], axis=1) for a, bb in zip(n_parts, w_parts))
                out = _dotp(n_parts, rhs)
                nmat[ch] = out[:, :2 * C]
                w[ch] = w[ch] + out[:, 2 * C:]

    o_parts, w_pair = [], []
    for p in pairs:
        oh = [_dotp(_split(mr[p, hd]), _split(w[p, hd])) for hd in range(2)]
        o_parts.append(r0[p] + jnp.where(head0_h, oh[0], oh[1]))
        w_pair.append(jnp.where(head0, w[p, 0], w[p, 1]))

    for p in pairs:
        for s in range(nseq):
            rows = slice(s * tq, (s + 1) * tq)
            ws = jnp.concatenate([w_pair[p][:C][rows], vv[rows, lanes(p)]], axis=0)
            zs = jnp.concatenate([zt[:C][rows, lanes(p)], zt[C:][rows, lanes(p)]], axis=0)
            if 2 * tq < LANES:
                pad = jnp.zeros((LANES - 2 * tq, LANES), F32)
                ws = jnp.concatenate([ws, pad], axis=0)
                zs = jnp.concatenate([zs, pad], axis=0)
            upd = _dotp(_split(ws.T), _split(zs))
            pe = p_end[s * tq:s * tq + 1, lanes(p)]
            s_sc[s, p] = s_sc[s, p] * pe + jnp.where(blockdiag, upd, 0.0)

    @pl.when(c == pl.num_programs(1) - 1)
    def _():
        sout_ref[...] = s_sc[...]

    o = jnp.concatenate(o_parts, axis=1)
    ones_blk = jnp.where(blockdiag, 1.0, 0.0).astype(BF16)
    mean = _head_sum(o, ones_blk) * (1.0 / B_HEAD)
    oc = o - mean
    var = _head_sum(oc * oc, ones_blk) * (1.0 / B_HEAD)
    on = oc * lax.rsqrt(var + RWKV_GN_EPS) * gng_ref[...] + gnb_ref[...] + bonus_ref[...]
    act_ref[...] = (on * g_ref[...]).astype(BF16)


def _rwkv_chunk(prep, gn_g, gn_b, s0, nseq, seq_len, row0, total_rows, into=None):
    r, kk, b, km, v, ld, g, bonus = prep
    rows = r.shape[0]
    C = RWKV_CHUNK
    n_groups = rows // (nseq * seq_len)
    n_chunks = nseq * seq_len // C
    t0 = row0 // C
    tok = pl.BlockSpec((C, B_WIDTH), lambda gi, c: (gi * n_chunks + c, 0))
    vec = pl.BlockSpec((1, B_WIDTH), lambda gi, c: (0, 0))
    st = pl.BlockSpec((nseq, B_PAIRS, LANES, LANES), lambda gi, c: (gi, 0, 0, 0))
    kern = functools.partial(_rwkv_chunk_kernel, nseq=nseq)
    extra = _into_args(into)
    return pl.pallas_call(
        _ignore_last_input(kern, 11) if into is not None else kern,
        out_shape=(jax.ShapeDtypeStruct((total_rows, B_WIDTH), BF16),
                   jax.ShapeDtypeStruct(s0.shape, F32)),
        grid=(n_groups, n_chunks),
        in_specs=[tok] * 8 + [vec, vec, st] + extra["specs"],
        out_specs=(pl.BlockSpec((C, B_WIDTH), lambda gi, c: (t0 + gi * n_chunks + c, 0)), st),
        scratch_shapes=[pltpu.VMEM((nseq, B_PAIRS, LANES, LANES), F32)],
        input_output_aliases={11: 0} if into is not None else {},
        compiler_params=_params("parallel", "arbitrary"),
        name="rwkv_chunk",
    )(r, kk, b, km, v, ld, g, bonus, gn_g, gn_b, s0, *extra["args"])


def _conv_kernel(z_ref, buf_ref, dw_ref, dwb_ref, lng_ref, lnb_ref, act_ref, tail_ref, ext_sc):
    tc = z_ref.shape[0]
    H = CONV_HALO

    @pl.when(pl.program_id(1) == 0)
    def _():
        ext_sc[0:H, :] = buf_ref[0]

    z = z_ref[...]
    ext_sc[H:H + tc, :] = z[:, :C_WIDTH] * jax.nn.sigmoid(z[:, C_WIDTH:])
    off = H - (CONV_W - 1)
    acc = jnp.zeros((tc, C_WIDTH), F32)
    for j in range(CONV_W):
        acc = acc + dw_ref[j:j + 1, :] * ext_sc[off + j:off + j + tc, :]
    y = _layernorm(acc + dwb_ref[...], lng_ref[...], lnb_ref[...])
    act_ref[...] = (y * jax.nn.sigmoid(y)).astype(BF16)
    tail = ext_sc[tc:tc + H, :]
    tail_ref[0] = tail
    ext_sc[0:H, :] = tail


def _conv(zc, buf, dw, dw_b, ln_g, ln_b, row0, n_seqs, seq_len, tc, into=None):
    tiles = seq_len // tc
    t0 = row0 // tc
    vec = pl.BlockSpec((1, C_WIDTH), lambda s, i: (0, 0))
    extra = _into_args(into)
    return pl.pallas_call(
        _ignore_last_input(_conv_kernel, 6) if into is not None else _conv_kernel,
        out_shape=(jax.ShapeDtypeStruct((zc.shape[0], C_WIDTH), BF16),
                   jax.ShapeDtypeStruct((n_seqs, CONV_HALO, C_WIDTH), F32)),
        grid=(n_seqs, tiles),
        in_specs=[
            pl.BlockSpec((tc, 2 * C_WIDTH), lambda s, i: (t0 + s * tiles + i, 0)),
            pl.BlockSpec((1, CONV_HALO, C_WIDTH), lambda s, i: (s, 0, 0)),
            pl.BlockSpec((CONV_HALO, C_WIDTH), lambda s, i: (0, 0)),
            vec, vec, vec,
        ] + extra["specs"],
        out_specs=(pl.BlockSpec((tc, C_WIDTH), lambda s, i: (t0 + s * tiles + i, 0)),
                   pl.BlockSpec((1, CONV_HALO, C_WIDTH), lambda s, i: (s, 0, 0))),
        scratch_shapes=[pltpu.VMEM((CONV_HALO + tc, C_WIDTH), F32)],
        input_output_aliases={6: 0} if into is not None else {},
        compiler_params=_params("parallel", "arbitrary"),
        name="conv",
    )(zc, buf, dw, dw_b, ln_g, ln_b, *extra["args"])


def _merge_kernel(x_ref, g_ref, wg0_ref, wg1_ref, wg2_ref, aa_ref, ab_ref, ac_ref,
                  woa_ref, wob_ref, woc_ref, wout_ref, o_ref, h_sc, acc_sc):
    j = pl.program_id(1)

    @pl.when(j == 0)
    def _():
        h_sc[...] = _rms(x_ref[...], g_ref[...]).astype(BF16)
        acc_sc[...] = jnp.zeros_like(acc_sc)

    h = h_sc[...]
    merged = jax.nn.sigmoid(_dot(h, wg0_ref[...])) * _dot(aa_ref[...], woa_ref[...])
    merged += jax.nn.sigmoid(_dot(h, wg1_ref[...])) * _dot(ab_ref[...], wob_ref[...])
    merged += jax.nn.sigmoid(_dot(h, wg2_ref[...])) * _dot(ac_ref[...], woc_ref[...])
    acc_sc[...] += _dot(merged.astype(BF16), wout_ref[...])

    @pl.when(j == pl.num_programs(1) - 1)
    def _():
        o_ref[...] = x_ref[...] + acc_sc[...]


def _merge(x, g, w_g, act_a, act_b, act_c, wo_a, wo_b, wo_c, w_out, tn=512):
    n, d = x.shape
    tm = ROW_TILE
    nj = d // tn
    row = lambda width: pl.BlockSpec((tm, width), lambda i, j: (i, 0))
    colw = lambda k: pl.BlockSpec((k, tn), lambda i, j: (0, j))
    return pl.pallas_call(
        _merge_kernel,
        out_shape=jax.ShapeDtypeStruct((n, d), F32),
        grid=(n // tm, nj),
        in_specs=[
            row(d),
            pl.BlockSpec((1, d), lambda i, j: (0, 0)),
            pl.BlockSpec((d, tn), lambda i, j: (0, j)),
            pl.BlockSpec((d, tn), lambda i, j: (0, nj + j)),
            pl.BlockSpec((d, tn), lambda i, j: (0, 2 * nj + j)),
            row(A_WIDTH), row(B_WIDTH), row(C_WIDTH),
            colw(A_WIDTH), colw(B_WIDTH), colw(C_WIDTH),
            pl.BlockSpec((tn, d), lambda i, j: (j, 0)),
        ],
        out_specs=row(d),
        scratch_shapes=[pltpu.VMEM((tm, d), BF16), pltpu.VMEM((tm, d), F32)],
        compiler_params=_params("parallel", "arbitrary"),
        name="merge",
    )(x, g, w_g, w_g, w_g, act_a, act_b, act_c, wo_a, wo_b, wo_c, w_out)


def _xattn_kernel(x_ref, g_ref, wq_ref, k_ref, v_ref, wo_ref, o_ref, *, nseq):
    x = x_ref[...]
    tq = x.shape[0]
    rows_per_seq = tq // nseq
    h = _rms(x, g_ref[...]).astype(BF16)
    q = _dot(h, wq_ref[...])
    scale = X_HEAD_DIM ** -0.5
    seq_out = []
    for s in range(nseq):
        rows = slice(s * rows_per_seq, (s + 1) * rows_per_seq)
        head_out = []
        for hd in range(X_HEADS):
            cols = slice(hd * X_HEAD_DIM, (hd + 1) * X_HEAD_DIM)
            kh = k_ref[s, :, cols].astype(BF16)
            vh = v_ref[s, :, cols].astype(BF16)
            sc = _dot_nt(q[rows, cols].astype(BF16), kh) * scale
            e = jnp.exp(sc - jnp.max(sc, axis=-1, keepdims=True))
            pr = e / jnp.sum(e, axis=-1, keepdims=True)
            head_out.append(_dot(pr.astype(BF16), vh))
        seq_out.append(jnp.concatenate(head_out, axis=1))
    o = seq_out[0] if nseq == 1 else jnp.concatenate(seq_out, axis=0)
    o_ref[...] = x + _dot(o.astype(BF16), wo_ref[...])


def _xattn(x, g, w_q, mem_k, mem_v, w_o, row0, nrows, seq_len, tq):
    d = x.shape[1]
    t0 = row0 // tq
    if seq_len >= tq:
        nseq = 1
        per_seq = seq_len // tq
        kv_idx = lambda i: (i // per_seq, 0, 0)
    else:
        nseq = tq // seq_len
        kv_idx = lambda i: (i, 0, 0)
    kv = pl.BlockSpec((nseq, N_MEM, X_WIDTH), kv_idx)
    return pl.pallas_call(
        functools.partial(_xattn_kernel, nseq=nseq),
        out_shape=jax.ShapeDtypeStruct(x.shape, F32),
        grid=(nrows // tq,),
        in_specs=[
            pl.BlockSpec((tq, d), lambda i: (i + t0, 0)),
            pl.BlockSpec((1, d), lambda i: (0, 0)),
            pl.BlockSpec((d, X_WIDTH), lambda i: (0, 0)),
            kv, kv,
            pl.BlockSpec((X_WIDTH, d), lambda i: (0, 0)),
        ],
        out_specs=pl.BlockSpec((tq, d), lambda i: (i + t0, 0)),
        input_output_aliases={0: 0},
        compiler_params=_params("parallel"),
        name="xattn",
    )(x, g, w_q, mem_k, mem_v, w_o)


def _final_norm_kernel(x_ref, g_ref, o_ref):
    o_ref[...] = _rms(x_ref[...], g_ref[...])


def _final_norm(x, g):
    n, d = x.shape
    tm = ROW_TILE
    return pl.pallas_call(
        _final_norm_kernel,
        out_shape=jax.ShapeDtypeStruct((n, d), F32),
        grid=(n // tm,),
        in_specs=[pl.BlockSpec((tm, d), lambda i: (i, 0)), pl.BlockSpec((1, d), lambda i: (0, 0))],
        out_specs=pl.BlockSpec((tm, d), lambda i: (i, 0)),
        compiler_params=_params("parallel"),
        name="final_norm",
    )(x, g)


def _pair_states(s):
    n = s.shape[0]
    s = s.reshape(n, B_PAIRS, 2 * B_HEAD, B_HEAD)
    return jnp.concatenate([s, s], axis=-1)


def _unpair_states(s):
    n = s.shape[0]
    top = s[:, :, :B_HEAD, :B_HEAD]
    bot = s[:, :, B_HEAD:, B_HEAD:]
    return jnp.stack([top, bot], axis=2).reshape(n, B_HEADS, B_HEAD, B_HEAD)


def _layer_params(l, w):
    bf = lambda a: a.astype(BF16)
    row = lambda a: a.reshape(1, -1)
    a_cols = 2 * A_WIDTH
    c0 = a_cols + B_COLS
    g0 = c0 + 2 * C_WIDTH
    w_in = w["w_in"][l]
    p = {}
    for name in ("n_ffn1", "n_mix", "n_x", "n_mem", "n_ffn2", "w0", "a0", "k_k", "k_a", "r_k",
                 "gn_g", "gn_b", "ln_a_g", "ln_a_b", "dw_b", "ln_c_g", "ln_c_b"):
        p[name] = row(w[name][l])
    for name in ("ffn1_w1", "ffn1_w3", "ffn1_w2", "ffn2_w1", "ffn2_w3", "ffn2_w2", "wo_a", "wo_b",
                 "wo_c", "w_out", "w_xq", "w_xo"):
        p[name] = bf(w[name][l])
    p["w_a"] = bf(w_in[:, :a_cols])
    p["w_b"] = bf(jnp.pad(w_in[:, a_cols:c0], ((0, 0), (0, B_COLS_PAD - B_COLS))))
    p["w_c"] = bf(w_in[:, c0:g0])
    p["w_g"] = bf(w_in[:, g0:])
    p["w_kv"] = bf(jnp.concatenate([w["w_xk"][l], w["w_xv"][l]], axis=1))
    p["mu_b"] = row(jnp.pad(w["mu_b"][l], (0, B_COLS_PAD - B_COLS)))
    wl = jnp.zeros((B_LORA_PAD, 3 * B_WIDTH), F32)
    wl = wl.at[0:64, 0:B_WIDTH].set(w["w2_decay"][l])
    wl = wl.at[64:128, B_WIDTH:2 * B_WIDTH].set(w["a2_aaa"][l])
    wl = wl.at[128:B_LORA, 2 * B_WIDTH:].set(w["g2_gate"][l])
    p["w_lora"] = bf(wl)
    p["dw"] = jnp.pad(w["dw"][l], ((0, CONV_HALO - CONV_W), (0, 0)))
    return p


def kernel(x_prompt, x_sample, mem_prompt, state_rwkv, state_shift, state_conv, cache_mem_k, cache_mem_v, n_ffn1, ffn1_w1, ffn1_w3, ffn1_w2, n_mix, w_in, mu_b, w0, w2_decay, a0, a2_aaa, g2_gate, k_k, k_a, r_k, gn_g, gn_b, wo_b, ln_a_g, ln_a_b, w_s, b_s, wo_a, dw, dw_b, ln_c_g, ln_c_b, wo_c, w_out, n_x, n_mem, w_xq, w_xk, w_xv, w_xo, n_ffn2, ffn2_w1, ffn2_w3, ffn2_w2, n_final):
    weights = dict(
        n_ffn1=n_ffn1, ffn1_w1=ffn1_w1, ffn1_w3=ffn1_w3, ffn1_w2=ffn1_w2, n_mix=n_mix, w_in=w_in,
        mu_b=mu_b, w0=w0, w2_decay=w2_decay, a0=a0, a2_aaa=a2_aaa, g2_gate=g2_gate, k_k=k_k,
        k_a=k_a, r_k=r_k, gn_g=gn_g, gn_b=gn_b, wo_b=wo_b, ln_a_g=ln_a_g, ln_a_b=ln_a_b,
        wo_a=wo_a, dw=dw, dw_b=dw_b, ln_c_g=ln_c_g, ln_c_b=ln_c_b, wo_c=wo_c,
        w_out=w_out, n_x=n_x, n_mem=n_mem, w_xq=w_xq, w_xk=w_xk, w_xv=w_xv, w_xo=w_xo,
        n_ffn2=n_ffn2, ffn2_w1=ffn2_w1, ffn2_w3=ffn2_w3, ffn2_w2=ffn2_w2)
    depth = w_in.shape[0]
    bp, tp, d = x_prompt.shape
    bs, ts, _ = x_sample.shape
    n_p, n_s = bp * tp, bs * ts
    x = jnp.concatenate([x_prompt.reshape(n_p, d), x_sample.reshape(n_s, d)], axis=0)
    mem = mem_prompt.reshape(bp * N_MEM, d)
    reps = CHUNK // ts

    outs = {k: [] for k in ("rwkv_p", "shift_p", "conv_p", "mk_p", "mv_p",
                            "rwkv_s", "shift_s", "conv_s", "gv_s")}
    for l in range(depth):
        p = _layer_params(l, weights)
        ws_all = jnp.stack([w_s[l], jnp.tile(w_s[l][:, :ts, :ts], (1, reps, reps))])
        bias_s = jnp.tile(b_s[l][:, :ts], (1, reps))
        bs_all = jnp.stack([jnp.broadcast_to(b_s[l][:, :, None], (A_GROUPS, CHUNK, CHUNK)),
                            jnp.broadcast_to(bias_s[:, :, None], (A_GROUPS, CHUNK, CHUNK))])

        kv = _norm_mm(mem, p["n_mem"], p["w_kv"], tn=2 * X_WIDTH)
        mk = kv[:, :X_WIDTH].reshape(bp, N_MEM, X_WIDTH)
        mv = kv[:, X_WIDTH:].reshape(bp, N_MEM, X_WIDTH)

        x = _ffn(x, p["n_ffn1"], p["ffn1_w1"], p["ffn1_w3"], p["ffn1_w2"])

        act_a, v_n = _gmlp(x, p["n_mix"], p["w_a"], p["ln_a_g"], p["ln_a_b"], ws_all, bs_all, n_p, ts)
        zb = _norm_mm(x, p["n_mix"], p["w_b"], tn=B_COLS_PAD // 3)
        zc = _norm_mm(x, p["n_mix"], p["w_c"], tn=2 * C_WIDTH)

        shift_exp = jnp.pad(state_shift[l][:, None, :],
                            ((0, 0), (0, ts - 1), (0, B_COLS_PAD - B_COLS))).reshape(n_s, B_COLS_PAD)
        prep_p = _rwkv_prep(zb, None, 0, n_p, tp, 256, p)
        prep_s = _rwkv_prep(zb, shift_exp, n_p, n_s, ts, 256, p)
        zero_state = jnp.zeros((bp, B_PAIRS, LANES, LANES), F32)
        act_b, s_p = _rwkv_chunk(prep_p, p["gn_g"], p["gn_b"], zero_state, 1, tp, 0, n_p + n_s)
        act_b, s_s = _rwkv_chunk(prep_s, p["gn_g"], p["gn_b"], _pair_states(state_rwkv[l]),
                                 RWKV_CHUNK // ts, ts, n_p, n_p + n_s, into=act_b)

        pad_buf = lambda b: jnp.pad(b, ((0, 0), (CONV_HALO - (CONV_W - 1), 0), (0, 0)))
        act_c, tail_p = _conv(zc, jnp.zeros((bp, CONV_HALO, C_WIDTH), F32), p["dw"], p["dw_b"],
                              p["ln_c_g"], p["ln_c_b"], 0, bp, tp, 256)
        act_c, tail_s = _conv(zc, pad_buf(state_conv[l]), p["dw"], p["dw_b"],
                              p["ln_c_g"], p["ln_c_b"], n_p, bs, ts, ts, into=act_c)

        x = _merge(x, p["n_mix"], p["w_g"], act_a, act_b, act_c,
                   p["wo_a"], p["wo_b"], p["wo_c"], p["w_out"])

        x = _xattn(x, p["n_x"], p["w_xq"], mk, mv, p["w_xo"], 0, n_p, tp, 256)
        x = _xattn(x, p["n_x"], p["w_xq"], cache_mem_k[l].reshape(bs, N_MEM, X_WIDTH),
                   cache_mem_v[l].reshape(bs, N_MEM, X_WIDTH), p["w_xo"], n_p, n_s, ts, 64)

        x = _ffn(x, p["n_ffn2"], p["ffn2_w1"], p["ffn2_w3"], p["ffn2_w2"])

        outs["rwkv_p"].append(_unpair_states(s_p))
        outs["shift_p"].append(zb[tp - 1:n_p:tp, :B_COLS])
        outs["conv_p"].append(tail_p[:, CONV_HALO - (CONV_W - 1):])
        outs["mk_p"].append(mk.reshape(bp, N_MEM, X_HEADS, X_HEAD_DIM))
        outs["mv_p"].append(mv.reshape(bp, N_MEM, X_HEADS, X_HEAD_DIM))
        outs["rwkv_s"].append(_unpair_states(s_s))
        outs["shift_s"].append(zb[n_p + ts - 1::ts, :B_COLS])
        outs["conv_s"].append(tail_s[:, CONV_HALO - (CONV_W - 1):])
        outs["gv_s"].append(v_n[n_p:].reshape(bs, ts, A_WIDTH))

    y = _final_norm(x, n_final.reshape(1, d))
    st = lambda k: jnp.stack(outs[k])
    return (y[:n_p].reshape(bp, tp, d), y[n_p:].reshape(bs, ts, d),
            st("rwkv_p"), st("shift_p"), st("conv_p"), st("mk_p"), st("mv_p"),
            st("rwkv_s"), st("shift_s"), st("conv_s"), st("gv_s"))
```

```python
import functools
import math

import jax
import jax.numpy as jnp
from jax import lax
from jax.experimental import pallas as pl
from jax.experimental.pallas import tpu as pltpu

F32 = jnp.float32
BF16 = jnp.bfloat16

LANES = 128
SUBLANES = 8
VMEM_LIMIT_BYTES = 56 * 1024 * 1024

D_MODEL = 2048
A_WIDTH = 512
A_GROUPS = 4
CHUNK = 128
B_WIDTH = 1024
B_HEAD = 64
B_HEADS = 16
B_PAIRS = B_HEADS // 2
B_LORA = 64 + 64 + 160
B_COLS = 3 * B_WIDTH + B_LORA
RWKV_GN_EPS = 64e-5
RWKV_CHUNK = 64
C_WIDTH = 512
CONV_W = 31
CONV_HALO = 32
N_MEM = 256
X_HEADS = 4
X_HEAD_DIM = 128
X_WIDTH = 512
EPS = 1e-6
LN_EPS = 1e-5

ROW_TILE = 512
COL_TILE = 512
B_COLS_PAD = -(-B_COLS // COL_TILE) * COL_TILE
B_LORA_PAD = B_COLS_PAD - 3 * B_WIDTH
SEG_A = 0
SEG_B = SEG_A + 2 * A_WIDTH
SEG_C = SEG_B + B_COLS_PAD
SEG_G = SEG_C + 2 * C_WIDTH

SCAN_MXU_PASSES = 1


def _params(*sem):
    return pltpu.CompilerParams(dimension_semantics=sem, vmem_limit_bytes=VMEM_LIMIT_BYTES)


def _layer_spec(l, block, index_fn):
    return pl.BlockSpec((None,) + tuple(block), lambda *g: (l,) + tuple(index_fn(*g)))


def _layer_vec(l, width):
    return _layer_spec(l, (1, width), lambda *g: (0, 0))


def _into_args(into):
    if into is None:
        return {"specs": [], "args": []}
    return {"specs": [pl.BlockSpec(memory_space=pl.ANY)], "args": [into]}


def _ignore_inputs(kernel_fn, start, count):
    def wrapped(*refs):
        return kernel_fn(*refs[:start], *refs[start + count:])
    return wrapped


def _rms(x, g):
    return x * lax.rsqrt(jnp.mean(x * x, axis=-1, keepdims=True) + EPS) * g


def _layernorm(x, g, b):
    xc = x - jnp.mean(x, axis=-1, keepdims=True)
    return xc * lax.rsqrt(jnp.mean(xc * xc, axis=-1, keepdims=True) + LN_EPS) * g + b


def _dot(a, b):
    return jnp.dot(a, b, preferred_element_type=F32)


def _dot_nt(a, b):
    return lax.dot_general(a, b, (((1,), (1,)), ((), ())), preferred_element_type=F32)


def _split2(x):
    hi = x.astype(BF16)
    lo = (x - hi.astype(F32)).astype(BF16)
    return hi, lo


def _split3(x):
    hi = x.astype(BF16)
    r1 = x - hi.astype(F32)
    mid = r1.astype(BF16)
    lo = (r1 - mid.astype(F32)).astype(BF16)
    return hi, mid, lo


def _split(x):
    return _split2(x) if SCAN_MXU_PASSES == 3 else (x.astype(BF16),)


def _dotp(a_parts, b_parts, dot=_dot):
    acc = dot(a_parts[0], b_parts[0])
    if len(a_parts) > 1:
        acc = acc + (dot(a_parts[0], b_parts[1]) + dot(a_parts[1], b_parts[0]))
    return acc


def _dot_sel(sel, x):
    sel = sel.astype(BF16)
    hi, mid, lo = _split3(x)
    return _dot(sel, hi) + (_dot(sel, mid) + _dot(sel, lo))


def _dot_sel_right(x, sel):
    hi, lo = _split2(x)
    return _dot(hi, sel) + _dot(lo, sel)


def _head_sum(x, ones_blk):
    cols = [_dot_sel_right(x[:, p * LANES:(p + 1) * LANES], ones_blk)
            for p in range(x.shape[1] // LANES)]
    return jnp.concatenate(cols, axis=1)


def _softplus(y):
    return jnp.maximum(y, 0.0) + jnp.log1p(jnp.exp(-jnp.abs(y)))


def _ffn_kernel(x_ref, g_ref, w1_ref, w3_ref, w2_ref, o_ref, h_sc, acc_sc):
    j = pl.program_id(1)

    @pl.when(j == 0)
    def _():
        h_sc[...] = _rms(x_ref[...], g_ref[...]).astype(BF16)
        acc_sc[...] = jnp.zeros_like(acc_sc)

    h = h_sc[...]
    a = _dot(h, w1_ref[...])
    b = _dot(h, w3_ref[...])
    act = (a * jax.nn.sigmoid(a) * b).astype(BF16)
    acc_sc[...] += _dot(act, w2_ref[...])

    @pl.when(j == pl.num_programs(1) - 1)
    def _():
        o_ref[...] = x_ref[...] + 0.5 * acc_sc[...]


def _ffn(x, l, g, w1, w3, w2):
    n, d = x.shape
    f = w1.shape[2]
    tm, tf = ROW_TILE, COL_TILE
    return pl.pallas_call(
        _ffn_kernel,
        out_shape=jax.ShapeDtypeStruct((n, d), F32),
        grid=(n // tm, f // tf),
        in_specs=[
            pl.BlockSpec((tm, d), lambda i, j: (i, 0)),
            _layer_vec(l, d),
            _layer_spec(l, (d, tf), lambda i, j: (0, j)),
            _layer_spec(l, (d, tf), lambda i, j: (0, j)),
            _layer_spec(l, (tf, d), lambda i, j: (j, 0)),
        ],
        out_specs=pl.BlockSpec((tm, d), lambda i, j: (i, 0)),
        scratch_shapes=[pltpu.VMEM((tm, d), BF16), pltpu.VMEM((tm, d), F32)],
        compiler_params=_params("parallel", "arbitrary"),
        name="ffn",
    )(x, g, w1, w3, w2)


def _norm_mm_kernel(x_ref, g_ref, w_ref, o_ref, h_sc):
    @pl.when(pl.program_id(1) == 0)
    def _():
        h_sc[...] = _rms(x_ref[...], g_ref[...]).astype(BF16)

    o_ref[...] = _dot(h_sc[...], w_ref[...])


def _norm_mm(x, l, g, w, col0, cols):
    n, d = x.shape
    tm, tn = min(ROW_TILE, n), COL_TILE
    c0 = col0 // tn
    return pl.pallas_call(
        _norm_mm_kernel,
        out_shape=jax.ShapeDtypeStruct((n, cols), F32),
        grid=(n // tm, cols // tn),
        in_specs=[
            pl.BlockSpec((tm, d), lambda i, j: (i, 0)),
            _layer_vec(l, d),
            _layer_spec(l, (d, tn), lambda i, j: (0, c0 + j)),
        ],
        out_specs=pl.BlockSpec((tm, tn), lambda i, j: (i, j)),
        scratch_shapes=[pltpu.VMEM((tm, d), BF16)],
        compiler_params=_params("parallel", "arbitrary"),
        name="norm_mm",
    )(x, g, w)


def _gmlp_kernel(x_ref, g_ref, w_ref, lng_ref, lnb_ref, ws_ref, bs_ref, act_ref, vn_ref, *,
                 n_prompt_tiles, short_len):
    is_sample = pl.program_id(0) >= n_prompt_tiles
    h = _rms(x_ref[...], g_ref[...]).astype(BF16)
    z = jax.nn.gelu(_dot(h, w_ref[...]))
    u = z[:, :A_WIDTH]
    v_n = _layernorm(z[:, A_WIDTH:], lng_ref[...], lnb_ref[...])
    vn_ref[...] = v_n

    row = lax.broadcasted_iota(jnp.int32, (CHUNK, CHUNK), 0)
    col = lax.broadcasted_iota(jnp.int32, (CHUNK, CHUNK), 1)
    shift = jnp.where(is_sample, int(math.log2(short_len)), int(math.log2(CHUNK)))
    shift = jnp.broadcast_to(shift.astype(jnp.int32), (CHUNK, CHUNK))
    same_seq = lax.shift_right_logical(row, shift) == lax.shift_right_logical(col, shift)
    keep = (col <= row) & same_seq
    v_b = v_n.astype(BF16)
    tm = x_ref.shape[0]
    gw = A_WIDTH // A_GROUPS
    for grp in range(A_GROUPS):
        ws = jnp.where(keep, ws_ref[0, grp], 0.0).astype(BF16)
        bias = bs_ref[0, grp]
        for c in range(tm // CHUNK):
            rows = slice(c * CHUNK, (c + 1) * CHUNK)
            cols = slice(grp * gw, (grp + 1) * gw)
            mixed = _dot(ws, v_b[rows, cols]) + bias
            act_ref[rows, cols] = (u[rows, cols] * mixed).astype(BF16)


def _gmlp(x, l, g, w_in, ln_g, ln_b, ws_all, bs_all, n_prompt, short_len):
    n, d = x.shape
    tm = ROW_TILE
    npt = n_prompt // tm
    kern = functools.partial(_gmlp_kernel, n_prompt_tiles=npt, short_len=short_len)
    sel = lambda i: (jnp.where(i >= npt, 1, 0), 0, 0, 0)
    return pl.pallas_call(
        kern,
        out_shape=(jax.ShapeDtypeStruct((n, A_WIDTH), BF16), jax.ShapeDtypeStruct((n, A_WIDTH), F32)),
        grid=(n // tm,),
        in_specs=[
            pl.BlockSpec((tm, d), lambda i: (i, 0)),
            _layer_vec(l, d),
            _layer_spec(l, (d, 2 * A_WIDTH), lambda i: (0, SEG_A // (2 * A_WIDTH))),
            _layer_vec(l, A_WIDTH),
            _layer_vec(l, A_WIDTH),
            _layer_spec(l, (1, A_GROUPS, CHUNK, CHUNK), sel),
            _layer_spec(l, (1, A_GROUPS, CHUNK, CHUNK), sel),
        ],
        out_specs=(pl.BlockSpec((tm, A_WIDTH), lambda i: (i, 0)),
                   pl.BlockSpec((tm, A_WIDTH), lambda i: (i, 0))),
        compiler_params=_params("parallel"),
        name="gmlp",
    )(x, g, w_in, ln_g, ln_b, ws_all, bs_all)


def _rwkv_prep_kernel(z_ref, prev_ref, mu_ref, w0_ref, a0_ref, kk_ref, ka_ref, rk_ref, wl_ref,
                      r_ref, kkn_ref, b_ref, km_ref, v_ref, ld_ref, g_ref, bonus_ref, *,
                      seq_len):
    tc = z_ref.shape[0]
    z = z_ref[...]
    row = lax.broadcasted_iota(jnp.int32, z.shape, 0)
    rolled = pltpu.roll(z, 1, 0)
    if seq_len >= tc:
        tiles_per_seq = seq_len // tc
        at_start = (pl.program_id(0) % tiles_per_seq) == 0
        last = prev_ref[SUBLANES - 1:SUBLANES, :]
        first = jnp.where(at_start, jnp.zeros_like(last), last)
        prev = jnp.where(row == 0, first, rolled)
    else:
        prev = jnp.where(row % seq_len == 0, prev_ref[...], rolled)
    zs = z + (prev - z) * mu_ref[...]

    r = zs[:, :B_WIDTH]
    k = zs[:, B_WIDTH:2 * B_WIDTH]
    v = zs[:, 2 * B_WIDTH:3 * B_WIDTH]
    lo0 = zs[:, 3 * B_WIDTH:3 * B_WIDTH + LANES]
    lane = lax.broadcasted_iota(jnp.int32, lo0.shape, 1)
    lo0 = jnp.where(lane < 64, jnp.tanh(lo0), lo0)
    lo_rest = jax.nn.sigmoid(zs[:, 3 * B_WIDTH + LANES:])
    lora_in = jnp.concatenate([lo0, lo_rest], axis=1).astype(BF16)
    lora = _dot(lora_in, wl_ref[...])
    lw = lora[:, :B_WIDTH]
    la = lora[:, B_WIDTH:2 * B_WIDTH]
    g = lora[:, 2 * B_WIDTH:]

    w_raw = -_softplus(-(w0_ref[...] + lw)) - 0.5
    ld = -jnp.exp(w_raw)
    a = jax.nn.sigmoid(a0_ref[...] + la)

    rl = lax.broadcasted_iota(jnp.int32, (LANES, LANES), 0)
    cl = lax.broadcasted_iota(jnp.int32, (LANES, LANES), 1)
    ones_blk = jnp.where((rl // B_HEAD) == (cl // B_HEAD), 1.0, 0.0).astype(BF16)

    kk = k * kk_ref[...]
    nrm = jnp.maximum(jnp.sqrt(_head_sum(kk * kk, ones_blk)), 1e-12)
    kk = kk / nrm
    k_mod = k * (1.0 + (a - 1.0) * ka_ref[...])
    bonus = _head_sum(r * k_mod * rk_ref[...], ones_blk) * v

    r_ref[...] = r
    kkn_ref[...] = kk
    b_ref[...] = kk * a
    km_ref[...] = k_mod
    v_ref[...] = v
    ld_ref[...] = ld
    g_ref[...] = g
    bonus_ref[...] = bonus


def _rwkv_prep(zb, prev, l, w, row0, nrows, seq_len, tc):
    t0 = row0 // tc
    if seq_len >= tc:
        per8 = tc // SUBLANES
        prev_spec = pl.BlockSpec(
            (SUBLANES, B_COLS_PAD), lambda i: (jnp.maximum((i + t0) * per8 - 1, 0), 0))
        prev_arr = zb
    else:
        prev_spec = pl.BlockSpec((tc, B_COLS_PAD), lambda i: (i, 0))
        prev_arr = prev
    out = jax.ShapeDtypeStruct((nrows, B_WIDTH), F32)
    return pl.pallas_call(
        functools.partial(_rwkv_prep_kernel, seq_len=seq_len),
        out_shape=(out,) * 8,
        grid=(nrows // tc,),
        in_specs=[
            pl.BlockSpec((tc, B_COLS_PAD), lambda i: (i + t0, 0)),
            prev_spec,
            _layer_vec(l, B_COLS_PAD), _layer_vec(l, B_WIDTH), _layer_vec(l, B_WIDTH),
            _layer_vec(l, B_WIDTH), _layer_vec(l, B_WIDTH), _layer_vec(l, B_WIDTH),
            _layer_spec(l, (B_LORA_PAD, 3 * B_WIDTH), lambda i: (0, 0)),
        ],
        out_specs=(pl.BlockSpec((tc, B_WIDTH), lambda i: (i, 0)),) * 8,
        compiler_params=_params("parallel"),
        name="rwkv_prep",
    )(zb, prev_arr, w["mu_b"], w["w0"], w["a0"], w["k_k"], w["k_a"], w["r_k"], w["w_lora"])


def _rwkv_chunk_kernel(r_ref, kk_ref, b_ref, km_ref, v_ref, ld_ref, g_ref, bonus_ref,
                       gng_ref, gnb_ref, s0_ref, act_ref, sout_ref, s_sc, *, nseq, zero_init):
    c = pl.program_id(1)
    C = RWKV_CHUNK
    tq = C // nseq
    rounds = int(math.log2(tq)) + 1
    pairs = range(B_PAIRS)
    lanes = lambda p: slice(p * LANES, (p + 1) * LANES)

    @pl.when(c == 0)
    def _():
        if zero_init:
            s_sc[...] = jnp.zeros_like(s_sc)
        else:
            rr = lax.broadcasted_iota(jnp.int32, (LANES, LANES), 0)
            cc = lax.broadcasted_iota(jnp.int32, (LANES, LANES), 1)
            for s in range(nseq):
                for p in pairs:
                    s_sc[s, p] = jnp.where((rr // B_HEAD) == (cc // B_HEAD), s0_ref[s, p], 0.0)

    row = lax.broadcasted_iota(jnp.int32, (2 * C, 2 * C), 0)
    col = lax.broadcasted_iota(jnp.int32, (2 * C, 2 * C), 1)
    t_i = row % C
    j_i = col % C
    mask_n = ((t_i // tq) == (j_i // tq)) & (j_i < t_i) & (row < C)
    blockdiag = (row // B_HEAD) == (col // B_HEAD)
    head0 = lax.broadcasted_iota(jnp.int32, (2 * C, LANES), 1) < B_HEAD
    t_h = lax.broadcasted_iota(jnp.int32, (C, 2 * C), 0)
    col_h = lax.broadcasted_iota(jnp.int32, (C, 2 * C), 1)
    j_h = col_h % C
    same_h = (t_h // tq) == (j_h // tq)
    mask_r = same_h & (j_h <= t_h)
    head0_h = lax.broadcasted_iota(jnp.int32, (C, LANES), 1) < B_HEAD

    ld = ld_ref[...]
    ld_pad = jnp.concatenate([ld, jnp.zeros_like(ld)], axis=0)
    tri = jnp.where(mask_r & (col_h < C), 1.0, 0.0)
    cum = _dot_sel(tri, ld_pad)
    if nseq == 1:
        cum_end = jnp.broadcast_to(cum[C - 1:C, :], cum.shape)
    else:
        cum_end = _dot_sel(jnp.where(same_h & (col_h < C), 1.0, 0.0), ld_pad)
    p_t = jnp.exp(cum)
    p_prev = jnp.exp(cum - ld)
    p_inv = jnp.exp(-cum)
    p_tail = jnp.exp(cum_end - cum)
    p_end = jnp.exp(cum_end)

    kk = kk_ref[...]
    b = b_ref[...]
    km = km_ref[...]
    vv = v_ref[...]
    am = -kk * p_prev
    rm = r_ref[...] * p_t
    x = jnp.concatenate([am, rm], axis=0)
    y = jnp.concatenate([b * p_inv, km * p_inv], axis=0)
    zt = jnp.concatenate([b * p_tail, km * p_tail], axis=0)


    x0, r0 = [], []
    for p in pairs:
        xs_parts, rs_parts = [], []
        for s in range(nseq):
            rows = slice(s * tq, (s + 1) * tq)
            xs = jnp.concatenate([am[rows, lanes(p)], rm[rows, lanes(p)]], axis=0)
            xs0 = _dotp(_split(xs), _split(s_sc[s, p]), dot=_dot_nt)
            xs_parts.append(xs0[:tq])
            rs_parts.append(xs0[tq:])
        x0.append(xs_parts[0] if nseq == 1 else jnp.concatenate(xs_parts, axis=0))
        r0.append(rs_parts[0] if nseq == 1 else jnp.concatenate(rs_parts, axis=0))

    chains = [(p, hd) for p in pairs for hd in range(2)]
    nmat, mr, w = {}, {}, {}
    for p in pairs:
        y_parts = _split(y[:, lanes(p)])
        for hd in range(2):
            xm = jnp.where(head0 if hd == 0 else ~head0, x[:, lanes(p)], 0.0)
            gm = _dotp(_split(xm), y_parts, dot=_dot_nt)
            nmat[p, hd] = jnp.where(mask_n, gm, 0.0)
            mr[p, hd] = jnp.where(mask_r, gm[C:], 0.0)
            w[p, hd] = jnp.concatenate([x0[p], vv[:, lanes(p)]], axis=0)

    for it in range(rounds):
        last = it + 1 == rounds
        for ch in chains:
            n_parts = _split(nmat[ch])
            w_parts = _split(w[ch])
            if last:
                w[ch] = w[ch] + _dotp(n_parts, w_parts)
            else:
                rhs = tuple(jnp.concatenate([a, bb], axis=1) for a, bb in zip(n_parts, w_parts))
                out = _dotp(n_parts, rhs)
                nmat[ch] = out[:, :2 * C]
                w[ch] = w[ch] + out[:, 2 * C:]

    o_parts, w_pair = [], []
    for p in pairs:
        oh = [_dotp(_split(mr[p, hd]), _split(w[p, hd])) for hd in range(2)]
        o_parts.append(r0[p] + jnp.where(head0_h, oh[0], oh[1]))
        w_pair.append(jnp.where(head0, w[p, 0], w[p, 1]))

    for p in pairs:
        for s in range(nseq):
            rows = slice(s * tq, (s + 1) * tq)
            ws = jnp.concatenate([w_pair[p][:C][rows], vv[rows, lanes(p)]], axis=0)
            zs = jnp.concatenate([zt[:C][rows, lanes(p)], zt[C:][rows, lanes(p)]], axis=0)
            if 2 * tq < LANES:
                pad = jnp.zeros((LANES - 2 * tq, LANES), F32)
                ws = jnp.concatenate([ws, pad], axis=0)
                zs = jnp.concatenate([zs, pad], axis=0)
            upd = _dotp(_split(ws.T), _split(zs))
            pe = p_end[s * tq:s * tq + 1, lanes(p)]
            s_sc[s, p] = s_sc[s, p] * pe + jnp.where(blockdiag, upd, 0.0)

    @pl.when(c == pl.num_programs(1) - 1)
    def _():
        sout_ref[...] = s_sc[...]

    o = jnp.concatenate(o_parts, axis=1)
    ones_blk = jnp.where(blockdiag, 1.0, 0.0).astype(BF16)
    mean = _head_sum(o, ones_blk) * (1.0 / B_HEAD)
    oc = o - mean
    var = _head_sum(oc * oc, ones_blk) * (1.0 / B_HEAD)
    on = oc * lax.rsqrt(var + RWKV_GN_EPS) * gng_ref[...] + gnb_ref[...] + bonus_ref[...]
    act_ref[...] = (on * g_ref[...]).astype(BF16)


def _rwkv_chunk(prep, l, w, s0, s_all, nseq, seq_len, row0, total_rows, act_into=None):
    r, kk, b, km, v, ld, g, bonus = prep
    rows = r.shape[0]
    C = RWKV_CHUNK
    n_seqs = rows // seq_len
    n_groups = n_seqs // nseq
    n_chunks = nseq * seq_len // C
    t0 = row0 // C
    n_layers = w["gn_g"].shape[0]
    tok = pl.BlockSpec((C, B_WIDTH), lambda gi, c: (gi * n_chunks + c, 0))
    st = _layer_spec(l, (nseq, B_PAIRS, LANES, LANES), lambda gi, c: (gi, 0, 0, 0))
    zero_init = s0 is None
    if zero_init:
        s0 = jnp.zeros((1, nseq, B_PAIRS, LANES, LANES), F32)
        s0_spec = pl.BlockSpec((None, nseq, B_PAIRS, LANES, LANES), lambda gi, c: (0, 0, 0, 0, 0))
    else:
        s0_spec = st
    kern = functools.partial(_rwkv_chunk_kernel, nseq=nseq, zero_init=zero_init)
    operands = [r, kk, b, km, v, ld, g, bonus, w["gn_g"], w["gn_b"], s0]
    in_specs = [tok] * 8 + [_layer_vec(l, B_WIDTH), _layer_vec(l, B_WIDTH), s0_spec]
    n_in = len(operands)
    aliases = {}
    for out_idx, into in ((0, act_into), (1, s_all)):
        if into is not None:
            aliases[len(operands)] = out_idx
            operands.append(into)
            in_specs.append(pl.BlockSpec(memory_space=pl.ANY))
    kern = _ignore_inputs(kern, n_in, len(aliases))
    return pl.pallas_call(
        kern,
        out_shape=(jax.ShapeDtypeStruct((total_rows, B_WIDTH), BF16),
                   jax.ShapeDtypeStruct((n_layers, n_seqs, B_PAIRS, LANES, LANES), F32)),
        grid=(n_groups, n_chunks),
        in_specs=in_specs,
        out_specs=(pl.BlockSpec((C, B_WIDTH), lambda gi, c: (t0 + gi * n_chunks + c, 0)), st),
        scratch_shapes=[pltpu.VMEM((nseq, B_PAIRS, LANES, LANES), F32)],
        input_output_aliases=aliases,
        compiler_params=_params("parallel", "arbitrary"),
        name="rwkv_chunk",
    )(*operands)


def _conv_kernel(z_ref, buf_ref, dw_ref, dwb_ref, lng_ref, lnb_ref, act_ref, tail_ref, ext_sc):
    tc = z_ref.shape[0]
    H = CONV_HALO

    @pl.when(pl.program_id(1) == 0)
    def _():
        ext_sc[0:H, :] = buf_ref[0]

    z = z_ref[...]
    ext_sc[H:H + tc, :] = z[:, :C_WIDTH] * jax.nn.sigmoid(z[:, C_WIDTH:])
    off = H - (CONV_W - 1)
    acc = jnp.zeros((tc, C_WIDTH), F32)
    for j in range(CONV_W):
        acc = acc + dw_ref[j:j + 1, :] * ext_sc[off + j:off + j + tc, :]
    y = _layernorm(acc + dwb_ref[...], lng_ref[...], lnb_ref[...])
    act_ref[...] = (y * jax.nn.sigmoid(y)).astype(BF16)
    tail = ext_sc[tc:tc + H, :]
    tail_ref[0] = tail
    ext_sc[0:H, :] = tail


def _conv(zc, buf, l, w, row0, n_seqs, seq_len, tc, into=None):
    tiles = seq_len // tc
    t0 = row0 // tc
    vec = _layer_vec(l, C_WIDTH)
    extra = _into_args(into)
    return pl.pallas_call(
        _ignore_inputs(_conv_kernel, 6, len(extra["args"])),
        out_shape=(jax.ShapeDtypeStruct((zc.shape[0], C_WIDTH), BF16),
                   jax.ShapeDtypeStruct((n_seqs, CONV_HALO, C_WIDTH), F32)),
        grid=(n_seqs, tiles),
        in_specs=[
            pl.BlockSpec((tc, 2 * C_WIDTH), lambda s, i: (t0 + s * tiles + i, 0)),
            pl.BlockSpec((1, CONV_HALO, C_WIDTH), lambda s, i: (s, 0, 0)),
            _layer_spec(l, (CONV_HALO, C_WIDTH), lambda s, i: (0, 0)),
            vec, vec, vec,
        ] + extra["specs"],
        out_specs=(pl.BlockSpec((tc, C_WIDTH), lambda s, i: (t0 + s * tiles + i, 0)),
                   pl.BlockSpec((1, CONV_HALO, C_WIDTH), lambda s, i: (s, 0, 0))),
        scratch_shapes=[pltpu.VMEM((CONV_HALO + tc, C_WIDTH), F32)],
        input_output_aliases={6: 0} if into is not None else {},
        compiler_params=_params("parallel", "arbitrary"),
        name="conv",
    )(zc, buf, w["dw"], w["dw_b"], w["ln_c_g"], w["ln_c_b"], *extra["args"])


def _merge_kernel(x_ref, g_ref, wg0_ref, wg1_ref, wg2_ref, aa_ref, ab_ref, ac_ref,
                  woa_ref, wob_ref, woc_ref, wout_ref, o_ref, h_sc, acc_sc):
    j = pl.program_id(1)

    @pl.when(j == 0)
    def _():
        h_sc[...] = _rms(x_ref[...], g_ref[...]).astype(BF16)
        acc_sc[...] = jnp.zeros_like(acc_sc)

    h = h_sc[...]
    merged = jax.nn.sigmoid(_dot(h, wg0_ref[...])) * _dot(aa_ref[...], woa_ref[...])
    merged += jax.nn.sigmoid(_dot(h, wg1_ref[...])) * _dot(ab_ref[...], wob_ref[...])
    merged += jax.nn.sigmoid(_dot(h, wg2_ref[...])) * _dot(ac_ref[...], woc_ref[...])
    acc_sc[...] += _dot(merged.astype(BF16), wout_ref[...])

    @pl.when(j == pl.num_programs(1) - 1)
    def _():
        o_ref[...] = x_ref[...] + acc_sc[...]


def _merge(x, l, w, act_a, act_b, act_c):
    n, d = x.shape
    tm, tn = ROW_TILE, COL_TILE
    nj = d // tn
    g0 = SEG_G // tn
    row = lambda width: pl.BlockSpec((tm, width), lambda i, j: (i, 0))
    colw = lambda k: _layer_spec(l, (k, tn), lambda i, j: (0, j))
    gate = lambda k: _layer_spec(l, (d, tn), lambda i, j: (0, g0 + k * nj + j))
    return pl.pallas_call(
        _merge_kernel,
        out_shape=jax.ShapeDtypeStruct((n, d), F32),
        grid=(n // tm, nj),
        in_specs=[
            row(d),
            _layer_vec(l, d),
            gate(0), gate(1), gate(2),
            row(A_WIDTH), row(B_WIDTH), row(C_WIDTH),
            colw(A_WIDTH), colw(B_WIDTH), colw(C_WIDTH),
            _layer_spec(l, (tn, d), lambda i, j: (j, 0)),
        ],
        out_specs=row(d),
        scratch_shapes=[pltpu.VMEM((tm, d), BF16), pltpu.VMEM((tm, d), F32)],
        compiler_params=_params("parallel", "arbitrary"),
        name="merge",
    )(x, w["n_mix"], w["w_in"], w["w_in"], w["w_in"], act_a, act_b, act_c,
      w["wo_a"], w["wo_b"], w["wo_c"], w["w_out"])


def _xattn_kernel(x_ref, g_ref, wq_ref, k_ref, v_ref, wo_ref, o_ref, *, nseq, head_rows):
    x = x_ref[...]
    tq = x.shape[0]
    rows_per_seq = tq // nseq
    h = _rms(x, g_ref[...]).astype(BF16)
    q = _dot(h, wq_ref[...])
    scale = X_HEAD_DIM ** -0.5
    seq_out = []
    for s in range(nseq):
        rows = slice(s * rows_per_seq, (s + 1) * rows_per_seq)
        head_out = []
        for hd in range(X_HEADS):
            cols = slice(hd * X_HEAD_DIM, (hd + 1) * X_HEAD_DIM)
            if head_rows:
                pick = pl.ds(s * N_MEM * X_HEADS + hd, N_MEM, stride=X_HEADS)
                kh = k_ref[pick, :].astype(BF16)
                vh = v_ref[pick, :].astype(BF16)
            else:
                kh = k_ref[s, :, cols].astype(BF16)
                vh = v_ref[s, :, cols].astype(BF16)
            sc = _dot_nt(q[rows, cols].astype(BF16), kh) * scale
            e = jnp.exp(sc - jnp.max(sc, axis=-1, keepdims=True))
            pr = e / jnp.sum(e, axis=-1, keepdims=True)
            head_out.append(_dot(pr.astype(BF16), vh))
        seq_out.append(jnp.concatenate(head_out, axis=1))
    o = seq_out[0] if nseq == 1 else jnp.concatenate(seq_out, axis=0)
    o_ref[...] = x + _dot(o.astype(BF16), wo_ref[...])


def _xattn(x, l, w, mem_k, mem_v, row0, nrows, seq_len, tq, kv_layer=None):
    d = x.shape[1]
    t0 = row0 // tq
    if seq_len >= tq:
        nseq = 1
        per_seq = seq_len // tq
        seq_of = lambda i: i // per_seq
    else:
        nseq = tq // seq_len
        seq_of = lambda i: i
    if kv_layer is None:
        kv = pl.BlockSpec((nseq, N_MEM, X_WIDTH), lambda i: (seq_of(i), 0, 0))
    else:
        steps = nrows // tq
        kv = pl.BlockSpec((nseq * N_MEM * X_HEADS, X_HEAD_DIM),
                          lambda i: (kv_layer * steps + seq_of(i), 0))
    return pl.pallas_call(
        functools.partial(_xattn_kernel, nseq=nseq, head_rows=kv_layer is not None),
        out_shape=jax.ShapeDtypeStruct(x.shape, F32),
        grid=(nrows // tq,),
        in_specs=[
            pl.BlockSpec((tq, d), lambda i: (i + t0, 0)),
            _layer_vec(l, d),
            _layer_spec(l, (d, X_WIDTH), lambda i: (0, 0)),
            kv, kv,
            _layer_spec(l, (X_WIDTH, d), lambda i: (0, 0)),
        ],
        out_specs=pl.BlockSpec((tq, d), lambda i: (i + t0, 0)),
        input_output_aliases={0: 0},
        compiler_params=_params("parallel"),
        name="xattn",
    )(x, w["n_x"], w["w_xq"], mem_k, mem_v, w["w_xo"])


def _final_norm_kernel(x_ref, g_ref, o_ref):
    o_ref[...] = _rms(x_ref[...], g_ref[...])


def _final_norm(x, g, row0, nrows):
    d = x.shape[1]
    tm = ROW_TILE
    t0 = row0 // tm
    return pl.pallas_call(
        _final_norm_kernel,
        out_shape=jax.ShapeDtypeStruct((nrows, d), F32),
        grid=(nrows // tm,),
        in_specs=[pl.BlockSpec((tm, d), lambda i: (i + t0, 0)), pl.BlockSpec((1, d), lambda i: (0, 0))],
        out_specs=pl.BlockSpec((tm, d), lambda i: (i, 0)),
        compiler_params=_params("parallel"),
        name="final_norm",
    )(x, g)


def _pair_states(s):
    lead = s.shape[:-3]
    s = s.reshape(lead + (B_PAIRS, 2 * B_HEAD, B_HEAD))
    return jnp.concatenate([s, s], axis=-1)


def _unpair_states(s):
    lead = s.shape[:-3]
    top = s[..., :B_HEAD, :B_HEAD]
    bot = s[..., B_HEAD:, B_HEAD:]
    return jnp.stack([top, bot], axis=-3).reshape(lead + (B_HEADS, B_HEAD, B_HEAD))


def _prepare_weights(w, ts):
    bf = lambda a: a.astype(BF16)
    vec = lambda a: a.reshape(a.shape[0], 1, -1)
    depth = w["w_in"].shape[0]
    a_cols = 2 * A_WIDTH
    c0 = a_cols + B_COLS
    g0 = c0 + 2 * C_WIDTH
    w_in = w["w_in"]
    p = {}
    for name in ("n_ffn1", "n_mix", "n_x", "n_mem", "n_ffn2", "w0", "a0", "k_k", "k_a", "r_k",
                 "gn_g", "gn_b", "ln_a_g", "ln_a_b", "dw_b", "ln_c_g", "ln_c_b"):
        p[name] = vec(w[name])
    for name in ("ffn1_w1", "ffn1_w3", "ffn1_w2", "ffn2_w1", "ffn2_w3", "ffn2_w2", "wo_a", "wo_b",
                 "wo_c", "w_out", "w_xq", "w_xo"):
        p[name] = bf(w[name])
    lane_pad = ((0, 0), (0, 0), (0, B_COLS_PAD - B_COLS))
    p["w_in"] = bf(jnp.concatenate(
        [w_in[:, :, :a_cols], jnp.pad(w_in[:, :, a_cols:c0], lane_pad), w_in[:, :, c0:]], axis=2))
    p["w_kv"] = bf(jnp.concatenate([w["w_xk"], w["w_xv"]], axis=2))
    p["mu_b"] = vec(jnp.pad(w["mu_b"], ((0, 0), (0, B_COLS_PAD - B_COLS))))
    wl = jnp.zeros((depth, B_LORA_PAD, 3 * B_WIDTH), F32)
    wl = wl.at[:, 0:64, 0:B_WIDTH].set(w["w2_decay"])
    wl = wl.at[:, 64:128, B_WIDTH:2 * B_WIDTH].set(w["a2_aaa"])
    wl = wl.at[:, 128:B_LORA, 2 * B_WIDTH:].set(w["g2_gate"])
    p["w_lora"] = bf(wl)
    p["dw"] = jnp.pad(w["dw"], ((0, 0), (0, CONV_HALO - CONV_W), (0, 0)))
    reps = CHUNK // ts
    w_s, b_s = w["w_s"], w["b_s"]
    p["ws_all"] = jnp.stack([w_s, jnp.tile(w_s[:, :, :ts, :ts], (1, 1, reps, reps))], axis=1)
    bias_s = jnp.tile(b_s[:, :, :ts], (1, 1, reps))
    full = (depth, A_GROUPS, CHUNK, CHUNK)
    p["bs_all"] = jnp.stack([jnp.broadcast_to(b_s[..., None], full),
                             jnp.broadcast_to(bias_s[..., None], full)], axis=1)
    return p


def kernel(x_prompt, x_sample, mem_prompt, state_rwkv, state_shift, state_conv, cache_mem_k, cache_mem_v, n_ffn1, ffn1_w1, ffn1_w3, ffn1_w2, n_mix, w_in, mu_b, w0, w2_decay, a0, a2_aaa, g2_gate, k_k, k_a, r_k, gn_g, gn_b, wo_b, ln_a_g, ln_a_b, w_s, b_s, wo_a, dw, dw_b, ln_c_g, ln_c_b, wo_c, w_out, n_x, n_mem, w_xq, w_xk, w_xv, w_xo, n_ffn2, ffn2_w1, ffn2_w3, ffn2_w2, n_final):
    weights = dict(
        n_ffn1=n_ffn1, ffn1_w1=ffn1_w1, ffn1_w3=ffn1_w3, ffn1_w2=ffn1_w2, n_mix=n_mix, w_in=w_in,
        mu_b=mu_b, w0=w0, w2_decay=w2_decay, a0=a0, a2_aaa=a2_aaa, g2_gate=g2_gate, k_k=k_k,
        k_a=k_a, r_k=r_k, gn_g=gn_g, gn_b=gn_b, wo_b=wo_b, ln_a_g=ln_a_g, ln_a_b=ln_a_b,
        w_s=w_s, b_s=b_s, wo_a=wo_a, dw=dw, dw_b=dw_b, ln_c_g=ln_c_g, ln_c_b=ln_c_b, wo_c=wo_c,
        w_out=w_out, n_x=n_x, n_mem=n_mem, w_xq=w_xq, w_xk=w_xk, w_xv=w_xv, w_xo=w_xo,
        n_ffn2=n_ffn2, ffn2_w1=ffn2_w1, ffn2_w3=ffn2_w3, ffn2_w2=ffn2_w2)
    depth = w_in.shape[0]
    bp, tp, d = x_prompt.shape
    bs, ts, _ = x_sample.shape
    n_p, n_s = bp * tp, bs * ts
    n = n_p + n_s
    w = _prepare_weights(weights, ts)
    x = jnp.concatenate([x_prompt.reshape(n_p, d), x_sample.reshape(n_s, d)], axis=0)
    mem = mem_prompt.reshape(bp * N_MEM, d)
    cache_k = cache_mem_k.reshape(depth * bs * N_MEM * X_HEADS, X_HEAD_DIM)
    cache_v = cache_mem_v.reshape(depth * bs * N_MEM * X_HEADS, X_HEAD_DIM)
    s0_sample = _pair_states(state_rwkv)
    pad_buf = lambda b: jnp.pad(b, ((0, 0), (CONV_HALO - (CONV_W - 1), 0), (0, 0)))
    zero_buf = jnp.zeros((bp, CONV_HALO, C_WIDTH), F32)

    outs = {k: [] for k in ("shift_p", "conv_p", "mk_p", "mv_p", "shift_s", "conv_s", "gv_s")}
    sp_all, ss_all = None, None
    for l in range(depth):
        kv = _norm_mm(mem, l, w["n_mem"], w["w_kv"], 0, 2 * X_WIDTH)
        mk = kv[:, :X_WIDTH].reshape(bp, N_MEM, X_WIDTH)
        mv = kv[:, X_WIDTH:].reshape(bp, N_MEM, X_WIDTH)

        x = _ffn(x, l, w["n_ffn1"], w["ffn1_w1"], w["ffn1_w3"], w["ffn1_w2"])

        act_a, v_n = _gmlp(x, l, w["n_mix"], w["w_in"], w["ln_a_g"], w["ln_a_b"],
                           w["ws_all"], w["bs_all"], n_p, ts)
        zb = _norm_mm(x, l, w["n_mix"], w["w_in"], SEG_B, B_COLS_PAD)
        zc = _norm_mm(x, l, w["n_mix"], w["w_in"], SEG_C, 2 * C_WIDTH)

        shift_exp = jnp.pad(state_shift[l][:, None, :],
                            ((0, 0), (0, ts - 1), (0, B_COLS_PAD - B_COLS))).reshape(n_s, B_COLS_PAD)
        prep_p = _rwkv_prep(zb, None, l, w, 0, n_p, tp, 256)
        prep_s = _rwkv_prep(zb, shift_exp, l, w, n_p, n_s, ts, 256)
        act_b, sp_all = _rwkv_chunk(prep_p, l, w, None, sp_all, 1, tp, 0, n)
        act_b, ss_all = _rwkv_chunk(prep_s, l, w, s0_sample, ss_all, RWKV_CHUNK // ts, ts, n_p, n,
                                    act_into=act_b)

        act_c, tail_p = _conv(zc, zero_buf, l, w, 0, bp, tp, 256)
        act_c, tail_s = _conv(zc, pad_buf(state_conv[l]), l, w, n_p, bs, ts, ts, into=act_c)

        x = _merge(x, l, w, act_a, act_b, act_c)

        x = _xattn(x, l, w, mk, mv, 0, n_p, tp, 256)
        x = _xattn(x, l, w, cache_k, cache_v, n_p, n_s, ts, 64, kv_layer=l)

        x = _ffn(x, l, w["n_ffn2"], w["ffn2_w1"], w["ffn2_w3"], w["ffn2_w2"])

        outs["shift_p"].append(zb[tp - 1:n_p:tp, :B_COLS])
        outs["conv_p"].append(tail_p[:, CONV_HALO - (CONV_W - 1):])
        outs["mk_p"].append(mk.reshape(bp, N_MEM, X_HEADS, X_HEAD_DIM))
        outs["mv_p"].append(mv.reshape(bp, N_MEM, X_HEADS, X_HEAD_DIM))
        outs["shift_s"].append(zb[n_p + ts - 1::ts, :B_COLS])
        outs["conv_s"].append(tail_s[:, CONV_HALO - (CONV_W - 1):])
        outs["gv_s"].append(v_n[n_p:].reshape(bs, ts, A_WIDTH))

    g_final = n_final.reshape(1, d)
    y_p = _final_norm(x, g_final, 0, n_p)
    y_s = _final_norm(x, g_final, n_p, n_s)
    st = lambda k: jnp.stack(outs[k])
    return (y_p.reshape(bp, tp, d), y_s.reshape(bs, ts, d),
            _unpair_states(sp_all), st("shift_p"), st("conv_p"), st("mk_p"), st("mv_p"),
            _unpair_states(ss_all), st("shift_s"), st("conv_s"), st("gv_s"))
```

```python
import functools
import math

import jax
import jax.numpy as jnp
from jax import lax
from jax.experimental import pallas as pl
from jax.experimental.pallas import tpu as pltpu

F32 = jnp.float32
BF16 = jnp.bfloat16

LANES = 128
SUBLANES = 8
VMEM_LIMIT_BYTES = 56 * 1024 * 1024

D_MODEL = 2048
A_WIDTH = 512
A_GROUPS = 4
CHUNK = 128
B_WIDTH = 1024
B_HEAD = 64
B_HEADS = 16
B_PAIRS = B_HEADS // 2
B_LORA = 64 + 64 + 160
B_COLS = 3 * B_WIDTH + B_LORA
RWKV_GN_EPS = 64e-5
RWKV_CHUNK = 64
C_WIDTH = 512
CONV_W = 31
CONV_HALO = 32
N_MEM = 256
X_HEADS = 4
X_HEAD_DIM = 128
X_WIDTH = 512
EPS = 1e-6
LN_EPS = 1e-5

ROW_TILE = 512
COL_TILE = 512
B_COLS_PAD = -(-B_COLS // COL_TILE) * COL_TILE
B_LORA_PAD = B_COLS_PAD - 3 * B_WIDTH

SCAN_MXU_PASSES = 1


def _params(*sem):
    return pltpu.CompilerParams(dimension_semantics=sem, vmem_limit_bytes=VMEM_LIMIT_BYTES)


def _layer_spec(l, block, index_fn):
    return pl.BlockSpec((None,) + tuple(block), lambda *g: (l,) + tuple(index_fn(*g)))


def _layer_vec(l, width):
    return _layer_spec(l, (1, width), lambda *g: (0, 0))


def _into_args(into):
    if into is None:
        return {"specs": [], "args": []}
    return {"specs": [pl.BlockSpec(memory_space=pl.ANY)], "args": [into]}


def _ignore_inputs(kernel_fn, start, count):
    def wrapped(*refs):
        return kernel_fn(*refs[:start], *refs[start + count:])
    return wrapped


def _rms(x, g):
    return x * lax.rsqrt(jnp.mean(x * x, axis=-1, keepdims=True) + EPS) * g


def _layernorm(x, g, b):
    xc = x - jnp.mean(x, axis=-1, keepdims=True)
    return xc * lax.rsqrt(jnp.mean(xc * xc, axis=-1, keepdims=True) + LN_EPS) * g + b


def _dot(a, b):
    return jnp.dot(a, b, preferred_element_type=F32)


def _dot_nt(a, b):
    return lax.dot_general(a, b, (((1,), (1,)), ((), ())), preferred_element_type=F32)


def _split2(x):
    hi = x.astype(BF16)
    lo = (x - hi.astype(F32)).astype(BF16)
    return hi, lo


def _split3(x):
    hi = x.astype(BF16)
    r1 = x - hi.astype(F32)
    mid = r1.astype(BF16)
    lo = (r1 - mid.astype(F32)).astype(BF16)
    return hi, mid, lo


def _split(x):
    return _split2(x) if SCAN_MXU_PASSES == 3 else (x.astype(BF16),)


def _dotp(a_parts, b_parts, dot=_dot):
    acc = dot(a_parts[0], b_parts[0])
    if len(a_parts) > 1:
        acc = acc + (dot(a_parts[0], b_parts[1]) + dot(a_parts[1], b_parts[0]))
    return acc


def _dot_sel(sel, x):
    sel = sel.astype(BF16)
    hi, mid, lo = _split3(x)
    return _dot(sel, hi) + (_dot(sel, mid) + _dot(sel, lo))


def _dot_sel_right(x, sel):
    hi, lo = _split2(x)
    return _dot(hi, sel) + _dot(lo, sel)


def _head_sum(x, ones_blk):
    cols = [_dot_sel_right(x[:, p * LANES:(p + 1) * LANES], ones_blk)
            for p in range(x.shape[1] // LANES)]
    return jnp.concatenate(cols, axis=1)


def _softplus(y):
    return jnp.maximum(y, 0.0) + jnp.log1p(jnp.exp(-jnp.abs(y)))


def _ffn_kernel(x_ref, g_ref, w1_ref, w3_ref, w2_ref, o_ref, h_sc, acc_sc):
    j = pl.program_id(1)

    @pl.when(j == 0)
    def _():
        h_sc[...] = _rms(x_ref[...], g_ref[...]).astype(BF16)
        acc_sc[...] = jnp.zeros_like(acc_sc)

    h = h_sc[...]
    a = _dot(h, w1_ref[...])
    b = _dot(h, w3_ref[...])
    act = (a * jax.nn.sigmoid(a) * b).astype(BF16)
    acc_sc[...] += _dot(act, w2_ref[...])

    @pl.when(j == pl.num_programs(1) - 1)
    def _():
        o_ref[...] = x_ref[...] + 0.5 * acc_sc[...]


def _ffn(x, l, g, w1, w3, w2):
    n, d = x.shape
    f = w1.shape[2]
    tm, tf = ROW_TILE, COL_TILE
    return pl.pallas_call(
        _ffn_kernel,
        out_shape=jax.ShapeDtypeStruct((n, d), F32),
        grid=(n // tm, f // tf),
        in_specs=[
            pl.BlockSpec((tm, d), lambda i, j: (i, 0)),
            _layer_vec(l, d),
            _layer_spec(l, (d, tf), lambda i, j: (0, j)),
            _layer_spec(l, (d, tf), lambda i, j: (0, j)),
            _layer_spec(l, (tf, d), lambda i, j: (j, 0)),
        ],
        out_specs=pl.BlockSpec((tm, d), lambda i, j: (i, 0)),
        scratch_shapes=[pltpu.VMEM((tm, d), BF16), pltpu.VMEM((tm, d), F32)],
        compiler_params=_params("parallel", "arbitrary"),
        name="ffn",
    )(x, g, w1, w3, w2)


def _norm_mm_kernel(x_ref, g_ref, w_ref, o_ref, h_sc):
    @pl.when(pl.program_id(1) == 0)
    def _():
        h_sc[...] = _rms(x_ref[...], g_ref[...]).astype(BF16)

    o_ref[...] = _dot(h_sc[...], w_ref[...])


def _norm_mm(x, l, g, w, tn):
    n, d = x.shape
    cols = w.shape[2]
    tm = min(ROW_TILE, n)
    return pl.pallas_call(
        _norm_mm_kernel,
        out_shape=jax.ShapeDtypeStruct((n, cols), F32),
        grid=(n // tm, cols // tn),
        in_specs=[
            pl.BlockSpec((tm, d), lambda i, j: (i, 0)),
            _layer_vec(l, d),
            _layer_spec(l, (d, tn), lambda i, j: (0, j)),
        ],
        out_specs=pl.BlockSpec((tm, tn), lambda i, j: (i, j)),
        scratch_shapes=[pltpu.VMEM((tm, d), BF16)],
        compiler_params=_params("parallel", "arbitrary"),
        name="norm_mm",
    )(x, g, w)


def _gmlp_kernel(x_ref, g_ref, w_ref, lng_ref, lnb_ref, ws_ref, bs_ref, act_ref, vn_ref, *,
                 n_prompt_tiles, short_len):
    is_sample = pl.program_id(0) >= n_prompt_tiles
    h = _rms(x_ref[...], g_ref[...]).astype(BF16)
    z = jax.nn.gelu(_dot(h, w_ref[...]))
    u = z[:, :A_WIDTH]
    v_n = _layernorm(z[:, A_WIDTH:], lng_ref[...], lnb_ref[...])
    vn_ref[...] = v_n

    row = lax.broadcasted_iota(jnp.int32, (CHUNK, CHUNK), 0)
    col = lax.broadcasted_iota(jnp.int32, (CHUNK, CHUNK), 1)
    shift = jnp.where(is_sample, int(math.log2(short_len)), int(math.log2(CHUNK)))
    shift = jnp.broadcast_to(shift.astype(jnp.int32), (CHUNK, CHUNK))
    same_seq = lax.shift_right_logical(row, shift) == lax.shift_right_logical(col, shift)
    keep = (col <= row) & same_seq
    v_b = v_n.astype(BF16)
    tm = x_ref.shape[0]
    gw = A_WIDTH // A_GROUPS
    for grp in range(A_GROUPS):
        ws = jnp.where(keep, ws_ref[0, grp], 0.0).astype(BF16)
        bias = bs_ref[0, grp]
        for c in range(tm // CHUNK):
            rows = slice(c * CHUNK, (c + 1) * CHUNK)
            cols = slice(grp * gw, (grp + 1) * gw)
            mixed = _dot(ws, v_b[rows, cols]) + bias
            act_ref[rows, cols] = (u[rows, cols] * mixed).astype(BF16)


def _gmlp(x, l, g, w_a, ln_g, ln_b, ws_all, bs_all, n_prompt, short_len):
    n, d = x.shape
    tm = ROW_TILE
    npt = n_prompt // tm
    kern = functools.partial(_gmlp_kernel, n_prompt_tiles=npt, short_len=short_len)
    sel = lambda i: (jnp.where(i >= npt, 1, 0), 0, 0, 0)
    return pl.pallas_call(
        kern,
        out_shape=(jax.ShapeDtypeStruct((n, A_WIDTH), BF16), jax.ShapeDtypeStruct((n, A_WIDTH), F32)),
        grid=(n // tm,),
        in_specs=[
            pl.BlockSpec((tm, d), lambda i: (i, 0)),
            _layer_vec(l, d),
            _layer_spec(l, (d, 2 * A_WIDTH), lambda i: (0, 0)),
            _layer_vec(l, A_WIDTH),
            _layer_vec(l, A_WIDTH),
            _layer_spec(l, (1, A_GROUPS, CHUNK, CHUNK), sel),
            _layer_spec(l, (1, A_GROUPS, CHUNK, CHUNK), sel),
        ],
        out_specs=(pl.BlockSpec((tm, A_WIDTH), lambda i: (i, 0)),
                   pl.BlockSpec((tm, A_WIDTH), lambda i: (i, 0))),
        compiler_params=_params("parallel"),
        name="gmlp",
    )(x, g, w_a, ln_g, ln_b, ws_all, bs_all)


def _rwkv_prep_kernel(z_ref, prev_ref, mu_ref, w0_ref, a0_ref, kk_ref, ka_ref, rk_ref, wl_ref,
                      r_ref, kkn_ref, b_ref, km_ref, v_ref, ld_ref, g_ref, bonus_ref, *,
                      seq_len):
    tc = z_ref.shape[0]
    z = z_ref[...]
    row = lax.broadcasted_iota(jnp.int32, z.shape, 0)
    rolled = pltpu.roll(z, 1, 0)
    if seq_len >= tc:
        tiles_per_seq = seq_len // tc
        at_start = (pl.program_id(0) % tiles_per_seq) == 0
        last = prev_ref[SUBLANES - 1:SUBLANES, :]
        first = jnp.where(at_start, jnp.zeros_like(last), last)
        prev = jnp.where(row == 0, first, rolled)
    else:
        prev = jnp.where(row % seq_len == 0, prev_ref[...], rolled)
    zs = z + (prev - z) * mu_ref[...]

    r = zs[:, :B_WIDTH]
    k = zs[:, B_WIDTH:2 * B_WIDTH]
    v = zs[:, 2 * B_WIDTH:3 * B_WIDTH]
    lo0 = zs[:, 3 * B_WIDTH:3 * B_WIDTH + LANES]
    lane = lax.broadcasted_iota(jnp.int32, lo0.shape, 1)
    lo0 = jnp.where(lane < 64, jnp.tanh(lo0), lo0)
    lo_rest = jax.nn.sigmoid(zs[:, 3 * B_WIDTH + LANES:])
    lora_in = jnp.concatenate([lo0, lo_rest], axis=1).astype(BF16)
    lora = _dot(lora_in, wl_ref[...])
    lw = lora[:, :B_WIDTH]
    la = lora[:, B_WIDTH:2 * B_WIDTH]
    g = lora[:, 2 * B_WIDTH:]

    w_raw = -_softplus(-(w0_ref[...] + lw)) - 0.5
    ld = -jnp.exp(w_raw)
    a = jax.nn.sigmoid(a0_ref[...] + la)

    rl = lax.broadcasted_iota(jnp.int32, (LANES, LANES), 0)
    cl = lax.broadcasted_iota(jnp.int32, (LANES, LANES), 1)
    ones_blk = jnp.where((rl // B_HEAD) == (cl // B_HEAD), 1.0, 0.0).astype(BF16)

    kk = k * kk_ref[...]
    nrm = jnp.maximum(jnp.sqrt(_head_sum(kk * kk, ones_blk)), 1e-12)
    kk = kk / nrm
    k_mod = k * (1.0 + (a - 1.0) * ka_ref[...])
    bonus = _head_sum(r * k_mod * rk_ref[...], ones_blk) * v

    r_ref[...] = r
    kkn_ref[...] = kk
    b_ref[...] = kk * a
    km_ref[...] = k_mod
    v_ref[...] = v
    ld_ref[...] = ld
    g_ref[...] = g
    bonus_ref[...] = bonus


def _rwkv_prep(zb, prev, l, w, row0, nrows, seq_len, tc):
    t0 = row0 // tc
    if seq_len >= tc:
        per8 = tc // SUBLANES
        prev_spec = pl.BlockSpec(
            (SUBLANES, B_COLS_PAD), lambda i: (jnp.maximum((i + t0) * per8 - 1, 0), 0))
        prev_arr = zb
    else:
        prev_spec = pl.BlockSpec((tc, B_COLS_PAD), lambda i: (i, 0))
        prev_arr = prev
    out = jax.ShapeDtypeStruct((nrows, B_WIDTH), F32)
    return pl.pallas_call(
        functools.partial(_rwkv_prep_kernel, seq_len=seq_len),
        out_shape=(out,) * 8,
        grid=(nrows // tc,),
        in_specs=[
            pl.BlockSpec((tc, B_COLS_PAD), lambda i: (i + t0, 0)),
            prev_spec,
            _layer_vec(l, B_COLS_PAD), _layer_vec(l, B_WIDTH), _layer_vec(l, B_WIDTH),
            _layer_vec(l, B_WIDTH), _layer_vec(l, B_WIDTH), _layer_vec(l, B_WIDTH),
            _layer_spec(l, (B_LORA_PAD, 3 * B_WIDTH), lambda i: (0, 0)),
        ],
        out_specs=(pl.BlockSpec((tc, B_WIDTH), lambda i: (i, 0)),) * 8,
        compiler_params=_params("parallel"),
        name="rwkv_prep",
    )(zb, prev_arr, w["mu_b"], w["w0"], w["a0"], w["k_k"], w["k_a"], w["r_k"], w["w_lora"])


def _rwkv_chunk_kernel(r_ref, kk_ref, b_ref, km_ref, v_ref, ld_ref, g_ref, bonus_ref,
                       gng_ref, gnb_ref, s0_ref, act_ref, sout_ref, s_sc, *, nseq, zero_init):
    c = pl.program_id(1)
    C = RWKV_CHUNK
    tq = C // nseq
    rounds = int(math.log2(tq)) + 1
    pairs = range(B_PAIRS)
    lanes = lambda p: slice(p * LANES, (p + 1) * LANES)

    @pl.when(c == 0)
    def _():
        if zero_init:
            s_sc[...] = jnp.zeros_like(s_sc)
        else:
            top = lax.broadcasted_iota(jnp.int32, (LANES, B_HEAD), 0) < B_HEAD
            for s in range(nseq):
                for p in pairs:
                    sc = s0_ref[s, p]
                    s_sc[s, p] = jnp.concatenate(
                        [jnp.where(top, sc, 0.0), jnp.where(top, 0.0, sc)], axis=1)

    row = lax.broadcasted_iota(jnp.int32, (2 * C, 2 * C), 0)
    col = lax.broadcasted_iota(jnp.int32, (2 * C, 2 * C), 1)
    t_i = row % C
    j_i = col % C
    mask_n = ((t_i // tq) == (j_i // tq)) & (j_i < t_i) & (row < C)
    blockdiag = (row // B_HEAD) == (col // B_HEAD)
    head0 = lax.broadcasted_iota(jnp.int32, (2 * C, LANES), 1) < B_HEAD
    t_h = lax.broadcasted_iota(jnp.int32, (C, 2 * C), 0)
    col_h = lax.broadcasted_iota(jnp.int32, (C, 2 * C), 1)
    j_h = col_h % C
    same_h = (t_h // tq) == (j_h // tq)
    mask_r = same_h & (j_h <= t_h)
    head0_h = lax.broadcasted_iota(jnp.int32, (C, LANES), 1) < B_HEAD

    ld = ld_ref[...]
    ld_pad = jnp.concatenate([ld, jnp.zeros_like(ld)], axis=0)
    tri = jnp.where(mask_r & (col_h < C), 1.0, 0.0)
    cum = _dot_sel(tri, ld_pad)
    if nseq == 1:
        cum_end = jnp.broadcast_to(cum[C - 1:C, :], cum.shape)
    else:
        cum_end = _dot_sel(jnp.where(same_h & (col_h < C), 1.0, 0.0), ld_pad)
    p_t = jnp.exp(cum)
    p_prev = jnp.exp(cum - ld)
    p_inv = jnp.exp(-cum)
    p_tail = jnp.exp(cum_end - cum)
    p_end = jnp.exp(cum_end)

    kk = kk_ref[...]
    b = b_ref[...]
    km = km_ref[...]
    vv = v_ref[...]
    am = -kk * p_prev
    rm = r_ref[...] * p_t
    x = jnp.concatenate([am, rm], axis=0)
    y = jnp.concatenate([b * p_inv, km * p_inv], axis=0)
    zt = jnp.concatenate([b * p_tail, km * p_tail], axis=0)


    x0, r0 = [], []
    for p in pairs:
        xs_parts, rs_parts = [], []
        for s in range(nseq):
            rows = slice(s * tq, (s + 1) * tq)
            xs = jnp.concatenate([am[rows, lanes(p)], rm[rows, lanes(p)]], axis=0)
            xs0 = _dotp(_split(xs), _split(s_sc[s, p]), dot=_dot_nt)
            xs_parts.append(xs0[:tq])
            rs_parts.append(xs0[tq:])
        x0.append(xs_parts[0] if nseq == 1 else jnp.concatenate(xs_parts, axis=0))
        r0.append(rs_parts[0] if nseq == 1 else jnp.concatenate(rs_parts, axis=0))

    chains = [(p, hd) for p in pairs for hd in range(2)]
    nmat, mr, w = {}, {}, {}
    for p in pairs:
        y_parts = _split(y[:, lanes(p)])
        for hd in range(2):
            xm = jnp.where(head0 if hd == 0 else ~head0, x[:, lanes(p)], 0.0)
            gm = _dotp(_split(xm), y_parts, dot=_dot_nt)
            nmat[p, hd] = jnp.where(mask_n, gm, 0.0)
            mr[p, hd] = jnp.where(mask_r, gm[C:], 0.0)
            w[p, hd] = jnp.concatenate([x0[p], vv[:, lanes(p)]], axis=0)

    for it in range(rounds):
        last = it + 1 == rounds
        for ch in chains:
            n_parts = _split(nmat[ch])
            w_parts = _split(w[ch])
            if last:
                w[ch] = w[ch] + _dotp(n_parts, w_parts)
            else:
                rhs = tuple(jnp.concatenate([a, bb], axis=1) for a, bb in zip(n_parts, w_parts))
                out = _dotp(n_parts, rhs)
                nmat[ch] = out[:, :2 * C]
                w[ch] = w[ch] + out[:, 2 * C:]

    o_parts, w_pair = [], []
    for p in pairs:
        oh = [_dotp(_split(mr[p, hd]), _split(w[p, hd])) for hd in range(2)]
        o_parts.append(r0[p] + jnp.where(head0_h, oh[0], oh[1]))
        w_pair.append(jnp.where(head0, w[p, 0], w[p, 1]))

    for p in pairs:
        for s in range(nseq):
            rows = slice(s * tq, (s + 1) * tq)
            ws = jnp.concatenate([w_pair[p][:C][rows], vv[rows, lanes(p)]], axis=0)
            zs = jnp.concatenate([zt[:C][rows, lanes(p)], zt[C:][rows, lanes(p)]], axis=0)
            if 2 * tq < LANES:
                pad = jnp.zeros((LANES - 2 * tq, LANES), F32)
                ws = jnp.concatenate([ws, pad], axis=0)
                zs = jnp.concatenate([zs, pad], axis=0)
            upd = _dotp(_split(ws.T), _split(zs))
            pe = p_end[s * tq:s * tq + 1, lanes(p)]
            s_sc[s, p] = s_sc[s, p] * pe + jnp.where(blockdiag, upd, 0.0)

    @pl.when(c == pl.num_programs(1) - 1)
    def _():
        top = lax.broadcasted_iota(jnp.int32, (LANES, LANES), 0) < B_HEAD
        for s in range(nseq):
            for p in pairs:
                full = s_sc[s, p]
                sout_ref[s, p] = jnp.where(top, full, pltpu.roll(full, B_HEAD, 1))[:, :B_HEAD]

    o = jnp.concatenate(o_parts, axis=1)
    ones_blk = jnp.where(blockdiag, 1.0, 0.0).astype(BF16)
    mean = _head_sum(o, ones_blk) * (1.0 / B_HEAD)
    oc = o - mean
    var = _head_sum(oc * oc, ones_blk) * (1.0 / B_HEAD)
    on = oc * lax.rsqrt(var + RWKV_GN_EPS) * gng_ref[...] + gnb_ref[...] + bonus_ref[...]
    act_ref[...] = (on * g_ref[...]).astype(BF16)


def _rwkv_chunk(prep, l, w, s0, s_all, nseq, seq_len, row0, total_rows, act_into=None):
    r, kk, b, km, v, ld, g, bonus = prep
    rows = r.shape[0]
    C = RWKV_CHUNK
    n_seqs = rows // seq_len
    n_groups = n_seqs // nseq
    n_chunks = nseq * seq_len // C
    t0 = row0 // C
    n_layers = w["gn_g"].shape[0]
    tok = pl.BlockSpec((C, B_WIDTH), lambda gi, c: (gi * n_chunks + c, 0))
    st = _layer_spec(l, (nseq, B_PAIRS, LANES, B_HEAD), lambda gi, c: (gi, 0, 0, 0))
    zero_init = s0 is None
    if zero_init:
        s0 = jnp.zeros((1, nseq, B_PAIRS, LANES, B_HEAD), F32)
        s0_spec = pl.BlockSpec((None, nseq, B_PAIRS, LANES, B_HEAD), lambda gi, c: (0, 0, 0, 0, 0))
    else:
        s0_spec = st
    kern = functools.partial(_rwkv_chunk_kernel, nseq=nseq, zero_init=zero_init)
    operands = [r, kk, b, km, v, ld, g, bonus, w["gn_g"], w["gn_b"], s0]
    in_specs = [tok] * 8 + [_layer_vec(l, B_WIDTH), _layer_vec(l, B_WIDTH), s0_spec]
    n_in = len(operands)
    aliases = {}
    for out_idx, into in ((0, act_into), (1, s_all)):
        if into is not None:
            aliases[len(operands)] = out_idx
            operands.append(into)
            in_specs.append(pl.BlockSpec(memory_space=pl.ANY))
    kern = _ignore_inputs(kern, n_in, len(aliases))
    return pl.pallas_call(
        kern,
        out_shape=(jax.ShapeDtypeStruct((total_rows, B_WIDTH), BF16),
                   jax.ShapeDtypeStruct((n_layers, n_seqs, B_PAIRS, LANES, B_HEAD), F32)),
        grid=(n_groups, n_chunks),
        in_specs=in_specs,
        out_specs=(pl.BlockSpec((C, B_WIDTH), lambda gi, c: (t0 + gi * n_chunks + c, 0)), st),
        scratch_shapes=[pltpu.VMEM((nseq, B_PAIRS, LANES, LANES), F32)],
        input_output_aliases=aliases,
        compiler_params=_params("parallel", "arbitrary"),
        name="rwkv_chunk",
    )(*operands)


def _conv_kernel(z_ref, buf_ref, dw_ref, dwb_ref, lng_ref, lnb_ref, act_ref, tail_ref, ext_sc, *,
                 nseq):
    tc = z_ref.shape[0] // nseq
    H = CONV_HALO
    span = H + tc

    @pl.when(pl.program_id(1) == 0)
    def _():
        for s in range(nseq):
            ext_sc[s * span:s * span + H, :] = buf_ref[s]

    z = z_ref[...]
    glu = z[:, :C_WIDTH] * jax.nn.sigmoid(z[:, C_WIDTH:])
    off = H - (CONV_W - 1)
    accs = []
    for s in range(nseq):
        base = s * span
        ext_sc[base + H:base + span, :] = glu[s * tc:(s + 1) * tc]
        acc = jnp.zeros((tc, C_WIDTH), F32)
        for j in range(CONV_W):
            acc = acc + dw_ref[j:j + 1, :] * ext_sc[base + off + j:base + off + j + tc, :]
        accs.append(acc)
        tail = ext_sc[base + tc:base + span, :]
        tail_ref[s] = tail
        ext_sc[base:base + H, :] = tail
    acc = accs[0] if nseq == 1 else jnp.concatenate(accs, axis=0)
    y = _layernorm(acc + dwb_ref[...], lng_ref[...], lnb_ref[...])
    act_ref[...] = (y * jax.nn.sigmoid(y)).astype(BF16)


def _conv(zc, buf, l, w, row0, n_seqs, seq_len, tc, nseq=1, into=None):
    tiles = seq_len // tc
    tc = tc * nseq
    t0 = row0 // tc
    vec = _layer_vec(l, C_WIDTH)
    extra = _into_args(into)
    return pl.pallas_call(
        _ignore_inputs(functools.partial(_conv_kernel, nseq=nseq), 6, len(extra["args"])),
        out_shape=(jax.ShapeDtypeStruct((zc.shape[0], C_WIDTH), BF16),
                   jax.ShapeDtypeStruct((n_seqs, CONV_HALO, C_WIDTH), F32)),
        grid=(n_seqs // nseq, tiles),
        in_specs=[
            pl.BlockSpec((tc, 2 * C_WIDTH), lambda s, i: (t0 + s * tiles + i, 0)),
            pl.BlockSpec((nseq, CONV_HALO, C_WIDTH), lambda s, i: (s, 0, 0)),
            _layer_spec(l, (CONV_HALO, C_WIDTH), lambda s, i: (0, 0)),
            vec, vec, vec,
        ] + extra["specs"],
        out_specs=(pl.BlockSpec((tc, C_WIDTH), lambda s, i: (t0 + s * tiles + i, 0)),
                   pl.BlockSpec((nseq, CONV_HALO, C_WIDTH), lambda s, i: (s, 0, 0))),
        scratch_shapes=[pltpu.VMEM((nseq * CONV_HALO + tc, C_WIDTH), F32)],
        input_output_aliases={6: 0} if into is not None else {},
        compiler_params=_params("parallel", "arbitrary"),
        name="conv",
    )(zc, buf, w["dw"], w["dw_b"], w["ln_c_g"], w["ln_c_b"], *extra["args"])


def _merge_kernel(x_ref, g_ref, wg0_ref, wg1_ref, wg2_ref, aa_ref, ab_ref, ac_ref,
                  woa_ref, wob_ref, woc_ref, wout_ref, o_ref, h_sc, acc_sc):
    j = pl.program_id(1)

    @pl.when(j == 0)
    def _():
        h_sc[...] = _rms(x_ref[...], g_ref[...]).astype(BF16)
        acc_sc[...] = jnp.zeros_like(acc_sc)

    h = h_sc[...]
    merged = jax.nn.sigmoid(_dot(h, wg0_ref[...])) * _dot(aa_ref[...], woa_ref[...])
    merged += jax.nn.sigmoid(_dot(h, wg1_ref[...])) * _dot(ab_ref[...], wob_ref[...])
    merged += jax.nn.sigmoid(_dot(h, wg2_ref[...])) * _dot(ac_ref[...], woc_ref[...])
    acc_sc[...] += _dot(merged.astype(BF16), wout_ref[...])

    @pl.when(j == pl.num_programs(1) - 1)
    def _():
        o_ref[...] = x_ref[...] + acc_sc[...]


def _merge(x, l, w, act_a, act_b, act_c):
    n, d = x.shape
    tm, tn = ROW_TILE, COL_TILE
    nj = d // tn
    row = lambda width: pl.BlockSpec((tm, width), lambda i, j: (i, 0))
    colw = lambda k: _layer_spec(l, (k, tn), lambda i, j: (0, j))
    gate = lambda k: _layer_spec(l, (d, tn), lambda i, j: (0, k * nj + j))
    return pl.pallas_call(
        _merge_kernel,
        out_shape=jax.ShapeDtypeStruct((n, d), F32),
        grid=(n // tm, nj),
        in_specs=[
            row(d),
            _layer_vec(l, d),
            gate(0), gate(1), gate(2),
            row(A_WIDTH), row(B_WIDTH), row(C_WIDTH),
            colw(A_WIDTH), colw(B_WIDTH), colw(C_WIDTH),
            _layer_spec(l, (tn, d), lambda i, j: (j, 0)),
        ],
        out_specs=row(d),
        scratch_shapes=[pltpu.VMEM((tm, d), BF16), pltpu.VMEM((tm, d), F32)],
        compiler_params=_params("parallel", "arbitrary"),
        name="merge",
    )(x, w["n_mix"], w["w_g"], w["w_g"], w["w_g"], act_a, act_b, act_c,
      w["wo_a"], w["wo_b"], w["wo_c"], w["w_out"])


def _xattn_kernel(x_ref, g_ref, wq_ref, k_ref, v_ref, wo_ref, o_ref, *, nseq, head_rows):
    x = x_ref[...]
    tq = x.shape[0]
    rows_per_seq = tq // nseq
    h = _rms(x, g_ref[...]).astype(BF16)
    q = _dot(h, wq_ref[...])
    scale = X_HEAD_DIM ** -0.5
    seq_out = []
    for s in range(nseq):
        rows = slice(s * rows_per_seq, (s + 1) * rows_per_seq)
        head_out = []
        for hd in range(X_HEADS):
            cols = slice(hd * X_HEAD_DIM, (hd + 1) * X_HEAD_DIM)
            if head_rows:
                pick = pl.ds(s * N_MEM * X_HEADS + hd, N_MEM, stride=X_HEADS)
                kh = k_ref[pick, :].astype(BF16)
                vh = v_ref[pick, :].astype(BF16)
            else:
                kh = k_ref[s, :, cols].astype(BF16)
                vh = v_ref[s, :, cols].astype(BF16)
            sc = _dot_nt(q[rows, cols].astype(BF16), kh) * scale
            e = jnp.exp(sc - jnp.max(sc, axis=-1, keepdims=True))
            pr = e / jnp.sum(e, axis=-1, keepdims=True)
            head_out.append(_dot(pr.astype(BF16), vh))
        seq_out.append(jnp.concatenate(head_out, axis=1))
    o = seq_out[0] if nseq == 1 else jnp.concatenate(seq_out, axis=0)
    o_ref[...] = x + _dot(o.astype(BF16), wo_ref[...])


def _xattn(x, l, w, mem_k, mem_v, row0, nrows, seq_len, tq, kv_layer=None):
    d = x.shape[1]
    t0 = row0 // tq
    if seq_len >= tq:
        nseq = 1
        per_seq = seq_len // tq
        seq_of = lambda i: i // per_seq
    else:
        nseq = tq // seq_len
        seq_of = lambda i: i
    if kv_layer is None:
        kv = pl.BlockSpec((nseq, N_MEM, X_WIDTH), lambda i: (seq_of(i), 0, 0))
    else:
        steps = nrows // tq
        kv = pl.BlockSpec((nseq * N_MEM * X_HEADS, X_HEAD_DIM),
                          lambda i: (kv_layer * steps + seq_of(i), 0))
    return pl.pallas_call(
        functools.partial(_xattn_kernel, nseq=nseq, head_rows=kv_layer is not None),
        out_shape=jax.ShapeDtypeStruct(x.shape, F32),
        grid=(nrows // tq,),
        in_specs=[
            pl.BlockSpec((tq, d), lambda i: (i + t0, 0)),
            _layer_vec(l, d),
            _layer_spec(l, (d, X_WIDTH), lambda i: (0, 0)),
            kv, kv,
            _layer_spec(l, (X_WIDTH, d), lambda i: (0, 0)),
        ],
        out_specs=pl.BlockSpec((tq, d), lambda i: (i + t0, 0)),
        input_output_aliases={0: 0},
        compiler_params=_params("parallel"),
        name="xattn",
    )(x, w["n_x"], w["w_xq"], mem_k, mem_v, w["w_xo"])


def _final_norm_kernel(x_ref, g_ref, o_ref):
    o_ref[...] = _rms(x_ref[...], g_ref[...])


def _final_norm(x, g, row0, nrows):
    d = x.shape[1]
    tm = ROW_TILE
    t0 = row0 // tm
    return pl.pallas_call(
        _final_norm_kernel,
        out_shape=jax.ShapeDtypeStruct((nrows, d), F32),
        grid=(nrows // tm,),
        in_specs=[pl.BlockSpec((tm, d), lambda i: (i + t0, 0)), pl.BlockSpec((1, d), lambda i: (0, 0))],
        out_specs=pl.BlockSpec((tm, d), lambda i: (i, 0)),
        compiler_params=_params("parallel"),
        name="final_norm",
    )(x, g)


def _prepare_weights(w, ts):
    bf = lambda a: a.astype(BF16)
    vec = lambda a: a.reshape(a.shape[0], 1, -1)
    depth = w["w_in"].shape[0]
    a_cols = 2 * A_WIDTH
    c0 = a_cols + B_COLS
    g0 = c0 + 2 * C_WIDTH
    w_in = w["w_in"]
    p = {}
    for name in ("n_ffn1", "n_mix", "n_x", "n_mem", "n_ffn2", "w0", "a0", "k_k", "k_a", "r_k",
                 "gn_g", "gn_b", "ln_a_g", "ln_a_b", "dw_b", "ln_c_g", "ln_c_b"):
        p[name] = vec(w[name])
    for name in ("ffn1_w1", "ffn1_w3", "ffn1_w2", "ffn2_w1", "ffn2_w3", "ffn2_w2", "wo_a", "wo_b",
                 "wo_c", "w_out", "w_xq", "w_xo"):
        p[name] = bf(w[name])
    p["w_a"] = bf(w_in[:, :, :a_cols])
    p["w_b"] = bf(jnp.pad(w_in[:, :, a_cols:c0], ((0, 0), (0, 0), (0, B_COLS_PAD - B_COLS))))
    p["w_c"] = bf(w_in[:, :, c0:g0])
    p["w_g"] = bf(w_in[:, :, g0:])
    p["w_kv"] = bf(jnp.concatenate([w["w_xk"], w["w_xv"]], axis=2))
    p["mu_b"] = vec(jnp.pad(w["mu_b"], ((0, 0), (0, B_COLS_PAD - B_COLS))))
    wl = jnp.zeros((depth, B_LORA_PAD, 3 * B_WIDTH), F32)
    wl = wl.at[:, 0:64, 0:B_WIDTH].set(w["w2_decay"])
    wl = wl.at[:, 64:128, B_WIDTH:2 * B_WIDTH].set(w["a2_aaa"])
    wl = wl.at[:, 128:B_LORA, 2 * B_WIDTH:].set(w["g2_gate"])
    p["w_lora"] = bf(wl)
    p["dw"] = jnp.pad(w["dw"], ((0, 0), (0, CONV_HALO - CONV_W), (0, 0)))
    reps = CHUNK // ts
    w_s, b_s = w["w_s"], w["b_s"]
    p["ws_all"] = jnp.stack([w_s, jnp.tile(w_s[:, :, :ts, :ts], (1, 1, reps, reps))], axis=1)
    bias_s = jnp.tile(b_s[:, :, :ts], (1, 1, reps))
    full = (depth, A_GROUPS, CHUNK, CHUNK)
    p["bs_all"] = jnp.stack([jnp.broadcast_to(b_s[..., None], full),
                             jnp.broadcast_to(bias_s[..., None], full)], axis=1)
    return p


def kernel(x_prompt, x_sample, mem_prompt, state_rwkv, state_shift, state_conv, cache_mem_k, cache_mem_v, n_ffn1, ffn1_w1, ffn1_w3, ffn1_w2, n_mix, w_in, mu_b, w0, w2_decay, a0, a2_aaa, g2_gate, k_k, k_a, r_k, gn_g, gn_b, wo_b, ln_a_g, ln_a_b, w_s, b_s, wo_a, dw, dw_b, ln_c_g, ln_c_b, wo_c, w_out, n_x, n_mem, w_xq, w_xk, w_xv, w_xo, n_ffn2, ffn2_w1, ffn2_w3, ffn2_w2, n_final):
    weights = dict(
        n_ffn1=n_ffn1, ffn1_w1=ffn1_w1, ffn1_w3=ffn1_w3, ffn1_w2=ffn1_w2, n_mix=n_mix, w_in=w_in,
        mu_b=mu_b, w0=w0, w2_decay=w2_decay, a0=a0, a2_aaa=a2_aaa, g2_gate=g2_gate, k_k=k_k,
        k_a=k_a, r_k=r_k, gn_g=gn_g, gn_b=gn_b, wo_b=wo_b, ln_a_g=ln_a_g, ln_a_b=ln_a_b,
        w_s=w_s, b_s=b_s, wo_a=wo_a, dw=dw, dw_b=dw_b, ln_c_g=ln_c_g, ln_c_b=ln_c_b, wo_c=wo_c,
        w_out=w_out, n_x=n_x, n_mem=n_mem, w_xq=w_xq, w_xk=w_xk, w_xv=w_xv, w_xo=w_xo,
        n_ffn2=n_ffn2, ffn2_w1=ffn2_w1, ffn2_w3=ffn2_w3, ffn2_w2=ffn2_w2)
    depth = w_in.shape[0]
    bp, tp, d = x_prompt.shape
    bs, ts, _ = x_sample.shape
    n_p, n_s = bp * tp, bs * ts
    n = n_p + n_s
    w = _prepare_weights(weights, ts)
    x = jnp.concatenate([x_prompt.reshape(n_p, d), x_sample.reshape(n_s, d)], axis=0)
    mem = mem_prompt.reshape(bp * N_MEM, d)
    cache_k = cache_mem_k.reshape(depth * bs * N_MEM * X_HEADS, X_HEAD_DIM)
    cache_v = cache_mem_v.reshape(depth * bs * N_MEM * X_HEADS, X_HEAD_DIM)
    s0_sample = state_rwkv.reshape(depth, bs, B_PAIRS, 2 * B_HEAD, B_HEAD)
    pad_buf = lambda b: jnp.pad(b, ((0, 0), (CONV_HALO - (CONV_W - 1), 0), (0, 0)))
    zero_buf = jnp.zeros((bp, CONV_HALO, C_WIDTH), F32)
    last_p = jnp.arange(bp, dtype=jnp.int32) * tp + (tp - 1)
    last_s = n_p + jnp.arange(bs, dtype=jnp.int32) * ts + (ts - 1)

    outs = {k: [] for k in ("shift_p", "conv_p", "mk_p", "mv_p", "shift_s", "conv_s", "gv_s")}
    sp_all, ss_all = None, None
    for l in range(depth):
        kv = _norm_mm(mem, l, w["n_mem"], w["w_kv"], 2 * X_WIDTH)
        mk = kv[:, :X_WIDTH].reshape(bp, N_MEM, X_WIDTH)
        mv = kv[:, X_WIDTH:].reshape(bp, N_MEM, X_WIDTH)

        x = _ffn(x, l, w["n_ffn1"], w["ffn1_w1"], w["ffn1_w3"], w["ffn1_w2"])

        act_a, v_n = _gmlp(x, l, w["n_mix"], w["w_a"], w["ln_a_g"], w["ln_a_b"],
                           w["ws_all"], w["bs_all"], n_p, ts)
        zb = _norm_mm(x, l, w["n_mix"], w["w_b"], B_COLS_PAD // 2)
        zc = _norm_mm(x, l, w["n_mix"], w["w_c"], 2 * C_WIDTH)

        shift_exp = jnp.pad(state_shift[l][:, None, :],
                            ((0, 0), (0, ts - 1), (0, B_COLS_PAD - B_COLS))).reshape(n_s, B_COLS_PAD)
        prep_p = _rwkv_prep(zb, None, l, w, 0, n_p, tp, 256)
        prep_s = _rwkv_prep(zb, shift_exp, l, w, n_p, n_s, ts, 256)
        act_b, sp_all = _rwkv_chunk(prep_p, l, w, None, sp_all, 1, tp, 0, n)
        act_b, ss_all = _rwkv_chunk(prep_s, l, w, s0_sample, ss_all, RWKV_CHUNK // ts, ts, n_p, n,
                                    act_into=act_b)

        act_c, tail_p = _conv(zc, zero_buf, l, w, 0, bp, tp, 256)
        act_c, tail_s = _conv(zc, pad_buf(state_conv[l]), l, w, n_p, bs, ts, ts, nseq=8, into=act_c)

        x = _merge(x, l, w, act_a, act_b, act_c)

        x = _xattn(x, l, w, mk, mv, 0, n_p, tp, ROW_TILE)
        x = _xattn(x, l, w, cache_k, cache_v, n_p, n_s, ts, 64, kv_layer=l)

        x = _ffn(x, l, w["n_ffn2"], w["ffn2_w1"], w["ffn2_w3"], w["ffn2_w2"])

        outs["shift_p"].append(jnp.take(zb, last_p, axis=0)[:, :B_COLS])
        outs["conv_p"].append(tail_p[:, CONV_HALO - (CONV_W - 1):])
        outs["mk_p"].append(mk.reshape(bp, N_MEM, X_HEADS, X_HEAD_DIM))
        outs["mv_p"].append(mv.reshape(bp, N_MEM, X_HEADS, X_HEAD_DIM))
        outs["shift_s"].append(jnp.take(zb, last_s, axis=0)[:, :B_COLS])
        outs["conv_s"].append(tail_s[:, CONV_HALO - (CONV_W - 1):])
        outs["gv_s"].append(v_n[n_p:].reshape(bs, ts, A_WIDTH))

    g_final = n_final.reshape(1, d)
    y_p = _final_norm(x, g_final, 0, n_p)
    y_s = _final_norm(x, g_final, n_p, n_s)
    st = lambda k: jnp.stack(outs[k])
    return (y_p.reshape(bp, tp, d), y_s.reshape(bs, ts, d),
            sp_all.reshape(depth, bp, B_HEADS, B_HEAD, B_HEAD), st("shift_p"), st("conv_p"), st("mk_p"), st("mv_p"),
            ss_all.reshape(depth, bs, B_HEADS, B_HEAD, B_HEAD), st("shift_s"), st("conv_s"), st("gv_s"))
```

```python
import functools
import math

import jax
import jax.numpy as jnp
from jax import lax
from jax.experimental import pallas as pl
from jax.experimental.pallas import tpu as pltpu

F32 = jnp.float32
BF16 = jnp.bfloat16

LANES = 128
SUBLANES = 8
VMEM_LIMIT_BYTES = 56 * 1024 * 1024

D_MODEL = 2048
A_WIDTH = 512
A_GROUPS = 4
CHUNK = 128
B_WIDTH = 1024
B_HEAD = 64
B_HEADS = 16
B_PAIRS = B_HEADS // 2
B_LORA = 64 + 64 + 160
B_COLS = 3 * B_WIDTH + B_LORA
RWKV_GN_EPS = 64e-5
RWKV_CHUNK = 64
C_WIDTH = 512
CONV_W = 31
CONV_HALO = 32
N_MEM = 256
X_HEADS = 4
X_HEAD_DIM = 128
X_WIDTH = 512
EPS = 1e-6
LN_EPS = 1e-5

ROW_TILE = 512
COL_TILE = 512
B_COLS_PAD = -(-B_COLS // COL_TILE) * COL_TILE
B_LORA_PAD = B_COLS_PAD - 3 * B_WIDTH

SCAN_MXU_PASSES = 1


def _params(*sem):
    return pltpu.CompilerParams(dimension_semantics=sem, vmem_limit_bytes=VMEM_LIMIT_BYTES)


def _layer_spec(l, block, index_fn):
    return pl.BlockSpec((None,) + tuple(block), lambda *g: (l,) + tuple(index_fn(*g)))


def _layer_vec(l, width):
    return _layer_spec(l, (1, width), lambda *g: (0, 0))


def _into_args(into):
    if into is None:
        return {"specs": [], "args": []}
    return {"specs": [pl.BlockSpec(memory_space=pl.ANY)], "args": [into]}


def _ignore_inputs(kernel_fn, start, count):
    def wrapped(*refs):
        return kernel_fn(*refs[:start], *refs[start + count:])
    return wrapped


def _rms(x, g):
    return x * lax.rsqrt(jnp.mean(x * x, axis=-1, keepdims=True) + EPS) * g


def _layernorm(x, g, b):
    xc = x - jnp.mean(x, axis=-1, keepdims=True)
    return xc * lax.rsqrt(jnp.mean(xc * xc, axis=-1, keepdims=True) + LN_EPS) * g + b


def _dot(a, b):
    return jnp.dot(a, b, preferred_element_type=F32)


def _dot_nt(a, b):
    return lax.dot_general(a, b, (((1,), (1,)), ((), ())), preferred_element_type=F32)


def _split2(x):
    hi = x.astype(BF16)
    lo = (x - hi.astype(F32)).astype(BF16)
    return hi, lo


def _split3(x):
    hi = x.astype(BF16)
    r1 = x - hi.astype(F32)
    mid = r1.astype(BF16)
    lo = (r1 - mid.astype(F32)).astype(BF16)
    return hi, mid, lo


def _split(x):
    return _split2(x) if SCAN_MXU_PASSES == 3 else (x.astype(BF16),)


def _dotp(a_parts, b_parts, dot=_dot):
    acc = dot(a_parts[0], b_parts[0])
    if len(a_parts) > 1:
        acc = acc + (dot(a_parts[0], b_parts[1]) + dot(a_parts[1], b_parts[0]))
    return acc


def _dot_sel(sel, x):
    sel = sel.astype(BF16)
    hi, mid, lo = _split3(x)
    return _dot(sel, hi) + (_dot(sel, mid) + _dot(sel, lo))


def _dot_sel_right(x, sel):
    hi, lo = _split2(x)
    return _dot(hi, sel) + _dot(lo, sel)


def _head_sum(x, ones_blk):
    cols = [_dot_sel_right(x[:, p * LANES:(p + 1) * LANES], ones_blk)
            for p in range(x.shape[1] // LANES)]
    return jnp.concatenate(cols, axis=1)


def _softplus(y):
    return jnp.maximum(y, 0.0) + jnp.log1p(jnp.exp(-jnp.abs(y)))


def _ffn_kernel(x_ref, g_ref, w1_ref, w3_ref, w2_ref, o_ref, h_sc, acc_sc):
    j = pl.program_id(1)

    @pl.when(j == 0)
    def _():
        h_sc[...] = _rms(x_ref[...], g_ref[...]).astype(BF16)
        acc_sc[...] = jnp.zeros_like(acc_sc)

    h = h_sc[...]
    a = _dot(h, w1_ref[...])
    b = _dot(h, w3_ref[...])
    act = (a * jax.nn.sigmoid(a) * b).astype(BF16)
    acc_sc[...] += _dot(act, w2_ref[...])

    @pl.when(j == pl.num_programs(1) - 1)
    def _():
        o_ref[...] = x_ref[...] + 0.5 * acc_sc[...]


def _ffn(x, l, g, w1, w3, w2):
    n, d = x.shape
    f = w1.shape[2]
    tm, tf = ROW_TILE, COL_TILE
    return pl.pallas_call(
        _ffn_kernel,
        out_shape=jax.ShapeDtypeStruct((n, d), F32),
        grid=(n // tm, f // tf),
        in_specs=[
            pl.BlockSpec((tm, d), lambda i, j: (i, 0)),
            _layer_vec(l, d),
            _layer_spec(l, (d, tf), lambda i, j: (0, j)),
            _layer_spec(l, (d, tf), lambda i, j: (0, j)),
            _layer_spec(l, (tf, d), lambda i, j: (j, 0)),
        ],
        out_specs=pl.BlockSpec((tm, d), lambda i, j: (i, 0)),
        scratch_shapes=[pltpu.VMEM((tm, d), BF16), pltpu.VMEM((tm, d), F32)],
        compiler_params=_params("parallel", "arbitrary"),
        name="ffn",
    )(x, g, w1, w3, w2)


def _norm_mm_kernel(x_ref, g_ref, w_ref, o_ref, h_sc):
    @pl.when(pl.program_id(1) == 0)
    def _():
        h_sc[...] = _rms(x_ref[...], g_ref[...]).astype(BF16)

    o_ref[...] = _dot(h_sc[...], w_ref[...])


def _norm_mm(x, l, g, w, tn):
    n, d = x.shape
    cols = w.shape[2]
    tm = min(ROW_TILE, n)
    return pl.pallas_call(
        _norm_mm_kernel,
        out_shape=jax.ShapeDtypeStruct((n, cols), F32),
        grid=(n // tm, cols // tn),
        in_specs=[
            pl.BlockSpec((tm, d), lambda i, j: (i, 0)),
            _layer_vec(l, d),
            _layer_spec(l, (d, tn), lambda i, j: (0, j)),
        ],
        out_specs=pl.BlockSpec((tm, tn), lambda i, j: (i, j)),
        scratch_shapes=[pltpu.VMEM((tm, d), BF16)],
        compiler_params=_params("parallel", "arbitrary"),
        name="norm_mm",
    )(x, g, w)


def _gmlp_kernel(x_ref, g_ref, w_ref, lng_ref, lnb_ref, ws_ref, bs_ref, act_ref, vn_ref, *,
                 n_prompt_tiles, short_len):
    is_sample = pl.program_id(0) >= n_prompt_tiles
    h = _rms(x_ref[...], g_ref[...]).astype(BF16)
    z = jax.nn.gelu(_dot(h, w_ref[...]))
    u = z[:, :A_WIDTH]
    v_n = _layernorm(z[:, A_WIDTH:], lng_ref[...], lnb_ref[...])
    vn_ref[...] = v_n

    row = lax.broadcasted_iota(jnp.int32, (CHUNK, CHUNK), 0)
    col = lax.broadcasted_iota(jnp.int32, (CHUNK, CHUNK), 1)
    shift = jnp.where(is_sample, int(math.log2(short_len)), int(math.log2(CHUNK)))
    shift = jnp.broadcast_to(shift.astype(jnp.int32), (CHUNK, CHUNK))
    same_seq = lax.shift_right_logical(row, shift) == lax.shift_right_logical(col, shift)
    keep = (col <= row) & same_seq
    v_b = v_n.astype(BF16)
    tm = x_ref.shape[0]
    gw = A_WIDTH // A_GROUPS
    for grp in range(A_GROUPS):
        ws = jnp.where(keep, ws_ref[0, grp], 0.0).astype(BF16)
        bias = bs_ref[0, grp]
        for c in range(tm // CHUNK):
            rows = slice(c * CHUNK, (c + 1) * CHUNK)
            cols = slice(grp * gw, (grp + 1) * gw)
            mixed = _dot(ws, v_b[rows, cols]) + bias
            act_ref[rows, cols] = (u[rows, cols] * mixed).astype(BF16)


def _gmlp(x, l, g, w_a, ln_g, ln_b, ws_all, bs_all, n_prompt, short_len):
    n, d = x.shape
    tm = ROW_TILE
    npt = n_prompt // tm
    kern = functools.partial(_gmlp_kernel, n_prompt_tiles=npt, short_len=short_len)
    sel = lambda i: (jnp.where(i >= npt, 1, 0), 0, 0, 0)
    return pl.pallas_call(
        kern,
        out_shape=(jax.ShapeDtypeStruct((n, A_WIDTH), BF16), jax.ShapeDtypeStruct((n, A_WIDTH), F32)),
        grid=(n // tm,),
        in_specs=[
            pl.BlockSpec((tm, d), lambda i: (i, 0)),
            _layer_vec(l, d),
            _layer_spec(l, (d, 2 * A_WIDTH), lambda i: (0, 0)),
            _layer_vec(l, A_WIDTH),
            _layer_vec(l, A_WIDTH),
            _layer_spec(l, (1, A_GROUPS, CHUNK, CHUNK), sel),
            _layer_spec(l, (1, A_GROUPS, CHUNK, CHUNK), sel),
        ],
        out_specs=(pl.BlockSpec((tm, A_WIDTH), lambda i: (i, 0)),
                   pl.BlockSpec((tm, A_WIDTH), lambda i: (i, 0))),
        compiler_params=_params("parallel"),
        name="gmlp",
    )(x, g, w_a, ln_g, ln_b, ws_all, bs_all)


def _rwkv_prep_kernel(z_ref, prev_ref, mu_ref, w0_ref, a0_ref, kk_ref, ka_ref, rk_ref, wl_ref,
                      r_ref, kkn_ref, b_ref, km_ref, v_ref, ld_ref, g_ref, bonus_ref, *,
                      seq_len):
    tc = z_ref.shape[0]
    z = z_ref[...]
    row = lax.broadcasted_iota(jnp.int32, z.shape, 0)
    rolled = pltpu.roll(z, 1, 0)
    if seq_len >= tc:
        tiles_per_seq = seq_len // tc
        at_start = (pl.program_id(0) % tiles_per_seq) == 0
        last = prev_ref[SUBLANES - 1:SUBLANES, :]
        first = jnp.where(at_start, jnp.zeros_like(last), last)
        prev = jnp.where(row == 0, first, rolled)
    else:
        prev = jnp.where(row % seq_len == 0, prev_ref[...], rolled)
    zs = z + (prev - z) * mu_ref[...]

    r = zs[:, :B_WIDTH]
    k = zs[:, B_WIDTH:2 * B_WIDTH]
    v = zs[:, 2 * B_WIDTH:3 * B_WIDTH]
    lo0 = zs[:, 3 * B_WIDTH:3 * B_WIDTH + LANES]
    lane = lax.broadcasted_iota(jnp.int32, lo0.shape, 1)
    lo0 = jnp.where(lane < 64, jnp.tanh(lo0), lo0)
    lo_rest = jax.nn.sigmoid(zs[:, 3 * B_WIDTH + LANES:])
    lora_in = jnp.concatenate([lo0, lo_rest], axis=1).astype(BF16)
    lora = _dot(lora_in, wl_ref[...])
    lw = lora[:, :B_WIDTH]
    la = lora[:, B_WIDTH:2 * B_WIDTH]
    g = lora[:, 2 * B_WIDTH:]

    w_raw = -_softplus(-(w0_ref[...] + lw)) - 0.5
    ld = -jnp.exp(w_raw)
    a = jax.nn.sigmoid(a0_ref[...] + la)

    rl = lax.broadcasted_iota(jnp.int32, (LANES, LANES), 0)
    cl = lax.broadcasted_iota(jnp.int32, (LANES, LANES), 1)
    ones_blk = jnp.where((rl // B_HEAD) == (cl // B_HEAD), 1.0, 0.0).astype(BF16)

    kk = k * kk_ref[...]
    nrm = jnp.maximum(jnp.sqrt(_head_sum(kk * kk, ones_blk)), 1e-12)
    kk = kk / nrm
    k_mod = k * (1.0 + (a - 1.0) * ka_ref[...])
    bonus = _head_sum(r * k_mod * rk_ref[...], ones_blk) * v

    r_ref[...] = r
    kkn_ref[...] = kk
    b_ref[...] = kk * a
    km_ref[...] = k_mod
    v_ref[...] = v
    ld_ref[...] = ld
    g_ref[...] = g
    bonus_ref[...] = bonus


def _rwkv_prep(zb, prev, l, w, row0, nrows, seq_len, tc):
    t0 = row0 // tc
    if seq_len >= tc:
        per8 = tc // SUBLANES
        prev_spec = pl.BlockSpec(
            (SUBLANES, B_COLS_PAD), lambda i: (jnp.maximum((i + t0) * per8 - 1, 0), 0))
        prev_arr = zb
    else:
        prev_spec = pl.BlockSpec((tc, B_COLS_PAD), lambda i: (i, 0))
        prev_arr = prev
    out = jax.ShapeDtypeStruct((nrows, B_WIDTH), F32)
    return pl.pallas_call(
        functools.partial(_rwkv_prep_kernel, seq_len=seq_len),
        out_shape=(out,) * 8,
        grid=(nrows // tc,),
        in_specs=[
            pl.BlockSpec((tc, B_COLS_PAD), lambda i: (i + t0, 0)),
            prev_spec,
            _layer_vec(l, B_COLS_PAD), _layer_vec(l, B_WIDTH), _layer_vec(l, B_WIDTH),
            _layer_vec(l, B_WIDTH), _layer_vec(l, B_WIDTH), _layer_vec(l, B_WIDTH),
            _layer_spec(l, (B_LORA_PAD, 3 * B_WIDTH), lambda i: (0, 0)),
        ],
        out_specs=(pl.BlockSpec((tc, B_WIDTH), lambda i: (i, 0)),) * 8,
        compiler_params=_params("parallel"),
        name="rwkv_prep",
    )(zb, prev_arr, w["mu_b"], w["w0"], w["a0"], w["k_k"], w["k_a"], w["r_k"], w["w_lora"])


def _rwkv_chunk_kernel(r_ref, kk_ref, b_ref, km_ref, v_ref, ld_ref, g_ref, bonus_ref,
                       gng_ref, gnb_ref, s0_ref, act_ref, sout_ref, s_sc, *, nseq, zero_init):
    c = pl.program_id(1)
    C = RWKV_CHUNK
    tq = C // nseq
    rounds = int(math.log2(tq)) + 1
    pairs = range(B_PAIRS)
    lanes = lambda p: slice(p * LANES, (p + 1) * LANES)

    @pl.when(c == 0)
    def _():
        if zero_init:
            s_sc[...] = jnp.zeros_like(s_sc)
        else:
            top = lax.broadcasted_iota(jnp.int32, (LANES, B_HEAD), 0) < B_HEAD
            for s in range(nseq):
                for p in pairs:
                    sc = s0_ref[s, p]
                    s_sc[s, p] = jnp.concatenate(
                        [jnp.where(top, sc, 0.0), jnp.where(top, 0.0, sc)], axis=1)

    row = lax.broadcasted_iota(jnp.int32, (2 * C, 2 * C), 0)
    col = lax.broadcasted_iota(jnp.int32, (2 * C, 2 * C), 1)
    t_i = row % C
    j_i = col % C
    mask_n = ((t_i // tq) == (j_i // tq)) & (j_i < t_i) & (row < C)
    blockdiag = (row // B_HEAD) == (col // B_HEAD)
    head0 = lax.broadcasted_iota(jnp.int32, (2 * C, LANES), 1) < B_HEAD
    t_h = lax.broadcasted_iota(jnp.int32, (C, 2 * C), 0)
    col_h = lax.broadcasted_iota(jnp.int32, (C, 2 * C), 1)
    j_h = col_h % C
    same_h = (t_h // tq) == (j_h // tq)
    mask_r = same_h & (j_h <= t_h)
    head0_h = lax.broadcasted_iota(jnp.int32, (C, LANES), 1) < B_HEAD

    ld = ld_ref[...]
    ld_pad = jnp.concatenate([ld, jnp.zeros_like(ld)], axis=0)
    tri = jnp.where(mask_r & (col_h < C), 1.0, 0.0)
    cum = _dot_sel(tri, ld_pad)
    if nseq == 1:
        cum_end = jnp.broadcast_to(cum[C - 1:C, :], cum.shape)
    else:
        cum_end = _dot_sel(jnp.where(same_h & (col_h < C), 1.0, 0.0), ld_pad)
    p_t = jnp.exp(cum)
    p_prev = jnp.exp(cum - ld)
    p_inv = jnp.exp(-cum)
    p_tail = jnp.exp(cum_end - cum)
    p_end = jnp.exp(cum_end)

    kk = kk_ref[...]
    b = b_ref[...]
    km = km_ref[...]
    vv = v_ref[...]
    am = -kk * p_prev
    rm = r_ref[...] * p_t
    x = jnp.concatenate([am, rm], axis=0)
    y = jnp.concatenate([b * p_inv, km * p_inv], axis=0)
    zt = jnp.concatenate([b * p_tail, km * p_tail], axis=0)


    x0, r0 = [], []
    for p in pairs:
        xs_parts, rs_parts = [], []
        for s in range(nseq):
            rows = slice(s * tq, (s + 1) * tq)
            xs = jnp.concatenate([am[rows, lanes(p)], rm[rows, lanes(p)]], axis=0)
            xs0 = _dotp(_split(xs), _split(s_sc[s, p]), dot=_dot_nt)
            xs_parts.append(xs0[:tq])
            rs_parts.append(xs0[tq:])
        x0.append(xs_parts[0] if nseq == 1 else jnp.concatenate(xs_parts, axis=0))
        r0.append(rs_parts[0] if nseq == 1 else jnp.concatenate(rs_parts, axis=0))

    chains = [(p, hd) for p in pairs for hd in range(2)]
    nmat, mr, w = {}, {}, {}
    for p in pairs:
        y_parts = _split(y[:, lanes(p)])
        for hd in range(2):
            xm = jnp.where(head0 if hd == 0 else ~head0, x[:, lanes(p)], 0.0)
            gm = _dotp(_split(xm), y_parts, dot=_dot_nt)
            nmat[p, hd] = jnp.where(mask_n, gm, 0.0)
            mr[p, hd] = jnp.where(mask_r, gm[C:], 0.0)
            w[p, hd] = jnp.concatenate([x0[p], vv[:, lanes(p)]], axis=0)

    for it in range(rounds):
        last = it + 1 == rounds
        for ch in chains:
            n_parts = _split(nmat[ch])
            w_parts = _split(w[ch])
            if last:
                w[ch] = w[ch] + _dotp(n_parts, w_parts)
            else:
                rhs = tuple(jnp.concatenate([a, bb], axis=1) for a, bb in zip(n_parts, w_parts))
                out = _dotp(n_parts, rhs)
                nmat[ch] = out[:, :2 * C]
                w[ch] = w[ch] + out[:, 2 * C:]

    o_parts, w_pair = [], []
    for p in pairs:
        oh = [_dotp(_split(mr[p, hd]), _split(w[p, hd])) for hd in range(2)]
        o_parts.append(r0[p] + jnp.where(head0_h, oh[0], oh[1]))
        w_pair.append(jnp.where(head0, w[p, 0], w[p, 1]))

    for p in pairs:
        for s in range(nseq):
            rows = slice(s * tq, (s + 1) * tq)
            ws = jnp.concatenate([w_pair[p][:C][rows], vv[rows, lanes(p)]], axis=0)
            zs = jnp.concatenate([zt[:C][rows, lanes(p)], zt[C:][rows, lanes(p)]], axis=0)
            if 2 * tq < LANES:
                pad = jnp.zeros((LANES - 2 * tq, LANES), F32)
                ws = jnp.concatenate([ws, pad], axis=0)
                zs = jnp.concatenate([zs, pad], axis=0)
            upd = _dotp(_split(ws.T), _split(zs))
            pe = p_end[s * tq:s * tq + 1, lanes(p)]
            s_sc[s, p] = s_sc[s, p] * pe + jnp.where(blockdiag, upd, 0.0)

    @pl.when(c == pl.num_programs(1) - 1)
    def _():
        top = lax.broadcasted_iota(jnp.int32, (LANES, LANES), 0) < B_HEAD
        for s in range(nseq):
            for p in pairs:
                full = s_sc[s, p]
                sout_ref[s, p] = jnp.where(top, full, pltpu.roll(full, B_HEAD, 1))[:, :B_HEAD]

    o = jnp.concatenate(o_parts, axis=1)
    ones_blk = jnp.where(blockdiag, 1.0, 0.0).astype(BF16)
    mean = _head_sum(o, ones_blk) * (1.0 / B_HEAD)
    oc = o - mean
    var = _head_sum(oc * oc, ones_blk) * (1.0 / B_HEAD)
    on = oc * lax.rsqrt(var + RWKV_GN_EPS) * gng_ref[...] + gnb_ref[...] + bonus_ref[...]
    act_ref[...] = (on * g_ref[...]).astype(BF16)


def _rwkv_chunk(prep, l, w, s0, s_all, nseq, seq_len, row0, total_rows, act_into=None):
    r, kk, b, km, v, ld, g, bonus = prep
    rows = r.shape[0]
    C = RWKV_CHUNK
    n_seqs = rows // seq_len
    n_groups = n_seqs // nseq
    n_chunks = nseq * seq_len // C
    t0 = row0 // C
    n_layers = w["gn_g"].shape[0]
    tok = pl.BlockSpec((C, B_WIDTH), lambda gi, c: (gi * n_chunks + c, 0))
    st = _layer_spec(l, (nseq, B_PAIRS, LANES, B_HEAD), lambda gi, c: (gi, 0, 0, 0))
    zero_init = s0 is None
    if zero_init:
        s0 = jnp.zeros((1, nseq, B_PAIRS, LANES, B_HEAD), F32)
        s0_spec = pl.BlockSpec((None, nseq, B_PAIRS, LANES, B_HEAD), lambda gi, c: (0, 0, 0, 0, 0))
    else:
        s0_spec = st
    kern = functools.partial(_rwkv_chunk_kernel, nseq=nseq, zero_init=zero_init)
    operands = [r, kk, b, km, v, ld, g, bonus, w["gn_g"], w["gn_b"], s0]
    in_specs = [tok] * 8 + [_layer_vec(l, B_WIDTH), _layer_vec(l, B_WIDTH), s0_spec]
    n_in = len(operands)
    aliases = {}
    for out_idx, into in ((0, act_into), (1, s_all)):
        if into is not None:
            aliases[len(operands)] = out_idx
            operands.append(into)
            in_specs.append(pl.BlockSpec(memory_space=pl.ANY))
    kern = _ignore_inputs(kern, n_in, len(aliases))
    return pl.pallas_call(
        kern,
        out_shape=(jax.ShapeDtypeStruct((total_rows, B_WIDTH), BF16),
                   jax.ShapeDtypeStruct((n_layers, n_seqs, B_PAIRS, LANES, B_HEAD), F32)),
        grid=(n_groups, n_chunks),
        in_specs=in_specs,
        out_specs=(pl.BlockSpec((C, B_WIDTH), lambda gi, c: (t0 + gi * n_chunks + c, 0)), st),
        scratch_shapes=[pltpu.VMEM((nseq, B_PAIRS, LANES, LANES), F32)],
        input_output_aliases=aliases,
        compiler_params=_params("parallel", "arbitrary"),
        name="rwkv_chunk",
    )(*operands)


def _conv_kernel(z_ref, buf_ref, dw_ref, dwb_ref, lng_ref, lnb_ref, act_ref, tail_ref, ext_sc, *,
                 nseq):
    tc = z_ref.shape[0] // nseq
    H = CONV_HALO
    span = H + tc

    @pl.when(pl.program_id(1) == 0)
    def _():
        for s in range(nseq):
            ext_sc[s * span:s * span + H, :] = buf_ref[s]

    z = z_ref[...]
    glu = z[:, :C_WIDTH] * jax.nn.sigmoid(z[:, C_WIDTH:])
    off = H - (CONV_W - 1)
    accs = []
    for s in range(nseq):
        base = s * span
        ext_sc[base + H:base + span, :] = glu[s * tc:(s + 1) * tc]
        acc = jnp.zeros((tc, C_WIDTH), F32)
        for j in range(CONV_W):
            acc = acc + dw_ref[j:j + 1, :] * ext_sc[base + off + j:base + off + j + tc, :]
        accs.append(acc)
        tail = ext_sc[base + tc:base + span, :]
        tail_ref[s] = tail
        ext_sc[base:base + H, :] = tail
    acc = accs[0] if nseq == 1 else jnp.concatenate(accs, axis=0)
    y = _layernorm(acc + dwb_ref[...], lng_ref[...], lnb_ref[...])
    act_ref[...] = (y * jax.nn.sigmoid(y)).astype(BF16)


def _conv(zc, buf, l, w, row0, n_seqs, seq_len, tc, nseq=1, into=None):
    tiles = seq_len // tc
    tc = tc * nseq
    t0 = row0 // tc
    vec = _layer_vec(l, C_WIDTH)
    extra = _into_args(into)
    return pl.pallas_call(
        _ignore_inputs(functools.partial(_conv_kernel, nseq=nseq), 6, len(extra["args"])),
        out_shape=(jax.ShapeDtypeStruct((zc.shape[0], C_WIDTH), BF16),
                   jax.ShapeDtypeStruct((n_seqs, CONV_HALO, C_WIDTH), F32)),
        grid=(n_seqs // nseq, tiles),
        in_specs=[
            pl.BlockSpec((tc, 2 * C_WIDTH), lambda s, i: (t0 + s * tiles + i, 0)),
            pl.BlockSpec((nseq, CONV_HALO, C_WIDTH), lambda s, i: (s, 0, 0)),
            _layer_spec(l, (CONV_HALO, C_WIDTH), lambda s, i: (0, 0)),
            vec, vec, vec,
        ] + extra["specs"],
        out_specs=(pl.BlockSpec((tc, C_WIDTH), lambda s, i: (t0 + s * tiles + i, 0)),
                   pl.BlockSpec((nseq, CONV_HALO, C_WIDTH), lambda s, i: (s, 0, 0))),
        scratch_shapes=[pltpu.VMEM((nseq * CONV_HALO + tc, C_WIDTH), F32)],
        input_output_aliases={6: 0} if into is not None else {},
        compiler_params=_params("parallel", "arbitrary"),
        name="conv",
    )(zc, buf, w["dw"], w["dw_b"], w["ln_c_g"], w["ln_c_b"], *extra["args"])


def _merge_kernel(x_ref, g_ref, wg0_ref, wg1_ref, wg2_ref, aa_ref, ab_ref, ac_ref,
                  woa_ref, wob_ref, woc_ref, wout_ref, o_ref, h_sc, acc_sc):
    j = pl.program_id(1)

    @pl.when(j == 0)
    def _():
        h_sc[...] = _rms(x_ref[...], g_ref[...]).astype(BF16)
        acc_sc[...] = jnp.zeros_like(acc_sc)

    h = h_sc[...]
    merged = jax.nn.sigmoid(_dot(h, wg0_ref[...])) * _dot(aa_ref[...], woa_ref[...])
    merged += jax.nn.sigmoid(_dot(h, wg1_ref[...])) * _dot(ab_ref[...], wob_ref[...])
    merged += jax.nn.sigmoid(_dot(h, wg2_ref[...])) * _dot(ac_ref[...], woc_ref[...])
    acc_sc[...] += _dot(merged.astype(BF16), wout_ref[...])

    @pl.when(j == pl.num_programs(1) - 1)
    def _():
        o_ref[...] = x_ref[...] + acc_sc[...]


def _merge(x, l, w, act_a, act_b, act_c):
    n, d = x.shape
    tm, tn = ROW_TILE, COL_TILE
    nj = d // tn
    row = lambda width: pl.BlockSpec((tm, width), lambda i, j: (i, 0))
    colw = lambda k: _layer_spec(l, (k, tn), lambda i, j: (0, j))
    gate = lambda k: _layer_spec(l, (d, tn), lambda i, j: (0, k * nj + j))
    return pl.pallas_call(
        _merge_kernel,
        out_shape=jax.ShapeDtypeStruct((n, d), F32),
        grid=(n // tm, nj),
        in_specs=[
            row(d),
            _layer_vec(l, d),
            gate(0), gate(1), gate(2),
            row(A_WIDTH), row(B_WIDTH), row(C_WIDTH),
            colw(A_WIDTH), colw(B_WIDTH), colw(C_WIDTH),
            _layer_spec(l, (tn, d), lambda i, j: (j, 0)),
        ],
        out_specs=row(d),
        scratch_shapes=[pltpu.VMEM((tm, d), BF16), pltpu.VMEM((tm, d), F32)],
        compiler_params=_params("parallel", "arbitrary"),
        name="merge",
    )(x, w["n_mix"], w["w_g"], w["w_g"], w["w_g"], act_a, act_b, act_c,
      w["wo_a"], w["wo_b"], w["wo_c"], w["w_out"])


def _xattn_kernel(x_ref, g_ref, wq_ref, k_ref, v_ref, wo_ref, o_ref, *, nseq, head_rows):
    x = x_ref[...]
    tq = x.shape[0]
    rows_per_seq = tq // nseq
    h = _rms(x, g_ref[...]).astype(BF16)
    q = _dot(h, wq_ref[...])
    scale = X_HEAD_DIM ** -0.5

    def memory(ref, s, hd):
        if head_rows:
            return ref[pl.ds(s * N_MEM * X_HEADS + hd, N_MEM, stride=X_HEADS), :].astype(BF16)
        return ref[s, :, hd * X_HEAD_DIM:(hd + 1) * X_HEAD_DIM].astype(BF16)

    units = [(s, hd) for s in range(nseq) for hd in range(X_HEADS)]
    scores = []
    for s, hd in units:
        qh = q[s * rows_per_seq:(s + 1) * rows_per_seq, hd * X_HEAD_DIM:(hd + 1) * X_HEAD_DIM]
        scores.append(_dot_nt(qh.astype(BF16), memory(k_ref, s, hd)) * scale)
    sc = jnp.concatenate(scores, axis=0)
    e = jnp.exp(sc - jnp.max(sc, axis=-1, keepdims=True))
    pr = e / jnp.sum(e, axis=-1, keepdims=True)
    outs = [_dot(pr[u * rows_per_seq:(u + 1) * rows_per_seq].astype(BF16), memory(v_ref, s, hd))
            for u, (s, hd) in enumerate(units)]
    seq_out = [jnp.concatenate(outs[s * X_HEADS:(s + 1) * X_HEADS], axis=1) for s in range(nseq)]
    o = seq_out[0] if nseq == 1 else jnp.concatenate(seq_out, axis=0)
    o_ref[...] = x + _dot(o.astype(BF16), wo_ref[...])


def _xattn(x, l, w, mem_k, mem_v, row0, nrows, seq_len, tq, kv_layer=None):
    d = x.shape[1]
    t0 = row0 // tq
    if seq_len >= tq:
        nseq = 1
        per_seq = seq_len // tq
        seq_of = lambda i: i // per_seq
    else:
        nseq = tq // seq_len
        seq_of = lambda i: i
    if kv_layer is None:
        kv = pl.BlockSpec((nseq, N_MEM, X_WIDTH), lambda i: (seq_of(i), 0, 0))
    else:
        steps = nrows // tq
        kv = pl.BlockSpec((nseq * N_MEM * X_HEADS, X_HEAD_DIM),
                          lambda i: (kv_layer * steps + seq_of(i), 0))
    return pl.pallas_call(
        functools.partial(_xattn_kernel, nseq=nseq, head_rows=kv_layer is not None),
        out_shape=jax.ShapeDtypeStruct(x.shape, F32),
        grid=(nrows // tq,),
        in_specs=[
            pl.BlockSpec((tq, d), lambda i: (i + t0, 0)),
            _layer_vec(l, d),
            _layer_spec(l, (d, X_WIDTH), lambda i: (0, 0)),
            kv, kv,
            _layer_spec(l, (X_WIDTH, d), lambda i: (0, 0)),
        ],
        out_specs=pl.BlockSpec((tq, d), lambda i: (i + t0, 0)),
        input_output_aliases={0: 0},
        compiler_params=_params("parallel"),
        name="xattn",
    )(x, w["n_x"], w["w_xq"], mem_k, mem_v, w["w_xo"])


def _final_norm_kernel(x_ref, g_ref, o_ref):
    o_ref[...] = _rms(x_ref[...], g_ref[...])


def _final_norm(x, g, row0, nrows):
    d = x.shape[1]
    tm = ROW_TILE
    t0 = row0 // tm
    return pl.pallas_call(
        _final_norm_kernel,
        out_shape=jax.ShapeDtypeStruct((nrows, d), F32),
        grid=(nrows // tm,),
        in_specs=[pl.BlockSpec((tm, d), lambda i: (i + t0, 0)), pl.BlockSpec((1, d), lambda i: (0, 0))],
        out_specs=pl.BlockSpec((tm, d), lambda i: (i, 0)),
        compiler_params=_params("parallel"),
        name="final_norm",
    )(x, g)


def _prepare_weights(w, ts):
    bf = lambda a: a.astype(BF16)
    vec = lambda a: a.reshape(a.shape[0], 1, -1)
    depth = w["w_in"].shape[0]
    a_cols = 2 * A_WIDTH
    c0 = a_cols + B_COLS
    g0 = c0 + 2 * C_WIDTH
    w_in = w["w_in"]
    p = {}
    for name in ("n_ffn1", "n_mix", "n_x", "n_mem", "n_ffn2", "w0", "a0", "k_k", "k_a", "r_k",
                 "gn_g", "gn_b", "ln_a_g", "ln_a_b", "dw_b", "ln_c_g", "ln_c_b"):
        p[name] = vec(w[name])
    for name in ("ffn1_w1", "ffn1_w3", "ffn1_w2", "ffn2_w1", "ffn2_w3", "ffn2_w2", "wo_a", "wo_b",
                 "wo_c", "w_out", "w_xq", "w_xo"):
        p[name] = bf(w[name])
    p["w_a"] = bf(w_in[:, :, :a_cols])
    p["w_b"] = bf(jnp.pad(w_in[:, :, a_cols:c0], ((0, 0), (0, 0), (0, B_COLS_PAD - B_COLS))))
    p["w_c"] = bf(w_in[:, :, c0:g0])
    p["w_g"] = bf(w_in[:, :, g0:])
    p["w_kv"] = bf(jnp.concatenate([w["w_xk"], w["w_xv"]], axis=2))
    p["mu_b"] = vec(jnp.pad(w["mu_b"], ((0, 0), (0, B_COLS_PAD - B_COLS))))
    wl = jnp.zeros((depth, B_LORA_PAD, 3 * B_WIDTH), F32)
    wl = wl.at[:, 0:64, 0:B_WIDTH].set(w["w2_decay"])
    wl = wl.at[:, 64:128, B_WIDTH:2 * B_WIDTH].set(w["a2_aaa"])
    wl = wl.at[:, 128:B_LORA, 2 * B_WIDTH:].set(w["g2_gate"])
    p["w_lora"] = bf(wl)
    p["dw"] = jnp.pad(w["dw"], ((0, 0), (0, CONV_HALO - CONV_W), (0, 0)))
    reps = CHUNK // ts
    w_s, b_s = w["w_s"], w["b_s"]
    p["ws_all"] = jnp.stack([w_s, jnp.tile(w_s[:, :, :ts, :ts], (1, 1, reps, reps))], axis=1)
    bias_s = jnp.tile(b_s[:, :, :ts], (1, 1, reps))
    full = (depth, A_GROUPS, CHUNK, CHUNK)
    p["bs_all"] = jnp.stack([jnp.broadcast_to(b_s[..., None], full),
                             jnp.broadcast_to(bias_s[..., None], full)], axis=1)
    return p


def kernel(x_prompt, x_sample, mem_prompt, state_rwkv, state_shift, state_conv, cache_mem_k, cache_mem_v, n_ffn1, ffn1_w1, ffn1_w3, ffn1_w2, n_mix, w_in, mu_b, w0, w2_decay, a0, a2_aaa, g2_gate, k_k, k_a, r_k, gn_g, gn_b, wo_b, ln_a_g, ln_a_b, w_s, b_s, wo_a, dw, dw_b, ln_c_g, ln_c_b, wo_c, w_out, n_x, n_mem, w_xq, w_xk, w_xv, w_xo, n_ffn2, ffn2_w1, ffn2_w3, ffn2_w2, n_final):
    weights = dict(
        n_ffn1=n_ffn1, ffn1_w1=ffn1_w1, ffn1_w3=ffn1_w3, ffn1_w2=ffn1_w2, n_mix=n_mix, w_in=w_in,
        mu_b=mu_b, w0=w0, w2_decay=w2_decay, a0=a0, a2_aaa=a2_aaa, g2_gate=g2_gate, k_k=k_k,
        k_a=k_a, r_k=r_k, gn_g=gn_g, gn_b=gn_b, wo_b=wo_b, ln_a_g=ln_a_g, ln_a_b=ln_a_b,
        w_s=w_s, b_s=b_s, wo_a=wo_a, dw=dw, dw_b=dw_b, ln_c_g=ln_c_g, ln_c_b=ln_c_b, wo_c=wo_c,
        w_out=w_out, n_x=n_x, n_mem=n_mem, w_xq=w_xq, w_xk=w_xk, w_xv=w_xv, w_xo=w_xo,
        n_ffn2=n_ffn2, ffn2_w1=ffn2_w1, ffn2_w3=ffn2_w3, ffn2_w2=ffn2_w2)
    depth = w_in.shape[0]
    bp, tp, d = x_prompt.shape
    bs, ts, _ = x_sample.shape
    n_p, n_s = bp * tp, bs * ts
    n = n_p + n_s
    w = _prepare_weights(weights, ts)
    x = jnp.concatenate([x_prompt.reshape(n_p, d), x_sample.reshape(n_s, d)], axis=0)
    mem = mem_prompt.reshape(bp * N_MEM, d)
    cache_k = cache_mem_k.reshape(depth * bs * N_MEM * X_HEADS, X_HEAD_DIM)
    cache_v = cache_mem_v.reshape(depth * bs * N_MEM * X_HEADS, X_HEAD_DIM)
    s0_sample = state_rwkv.reshape(depth, bs, B_PAIRS, 2 * B_HEAD, B_HEAD)
    pad_buf = lambda b: jnp.pad(b, ((0, 0), (CONV_HALO - (CONV_W - 1), 0), (0, 0)))
    zero_buf = jnp.zeros((bp, CONV_HALO, C_WIDTH), F32)
    last_p = jnp.arange(bp, dtype=jnp.int32) * tp + (tp - 1)
    last_s = n_p + jnp.arange(bs, dtype=jnp.int32) * ts + (ts - 1)

    outs = {k: [] for k in ("shift_p", "conv_p", "mk_p", "mv_p", "shift_s", "conv_s", "gv_s")}
    sp_all = jnp.zeros((depth, bp, B_PAIRS, 2 * B_HEAD, B_HEAD), F32)
    ss_all = jnp.zeros((depth, bs, B_PAIRS, 2 * B_HEAD, B_HEAD), F32)
    act_b = jnp.zeros((n, B_WIDTH), BF16)
    act_c = jnp.zeros((n, C_WIDTH), BF16)
    for l in range(depth):
        kv = _norm_mm(mem, l, w["n_mem"], w["w_kv"], 2 * X_WIDTH)
        mk = kv[:, :X_WIDTH].reshape(bp, N_MEM, X_WIDTH)
        mv = kv[:, X_WIDTH:].reshape(bp, N_MEM, X_WIDTH)

        x = _ffn(x, l, w["n_ffn1"], w["ffn1_w1"], w["ffn1_w3"], w["ffn1_w2"])

        act_a, v_n = _gmlp(x, l, w["n_mix"], w["w_a"], w["ln_a_g"], w["ln_a_b"],
                           w["ws_all"], w["bs_all"], n_p, ts)
        zb = _norm_mm(x, l, w["n_mix"], w["w_b"], B_COLS_PAD // 2)
        zc = _norm_mm(x, l, w["n_mix"], w["w_c"], 2 * C_WIDTH)

        shift_exp = jnp.pad(state_shift[l][:, None, :],
                            ((0, 0), (0, ts - 1), (0, B_COLS_PAD - B_COLS))).reshape(n_s, B_COLS_PAD)
        prep_p = _rwkv_prep(zb, None, l, w, 0, n_p, tp, 256)
        prep_s = _rwkv_prep(zb, shift_exp, l, w, n_p, n_s, ts, 256)
        act_b, sp_all = _rwkv_chunk(prep_p, l, w, None, sp_all, 1, tp, 0, n, act_into=act_b)
        act_b, ss_all = _rwkv_chunk(prep_s, l, w, s0_sample, ss_all, RWKV_CHUNK // ts, ts, n_p, n,
                                    act_into=act_b)

        act_c, tail_p = _conv(zc, zero_buf, l, w, 0, bp, tp, 256, into=act_c)
        act_c, tail_s = _conv(zc, pad_buf(state_conv[l]), l, w, n_p, bs, ts, ts, nseq=8, into=act_c)

        x = _merge(x, l, w, act_a, act_b, act_c)

        x = _xattn(x, l, w, mk, mv, 0, n_p, tp, ROW_TILE)
        x = _xattn(x, l, w, cache_k, cache_v, n_p, n_s, ts, 64, kv_layer=l)

        x = _ffn(x, l, w["n_ffn2"], w["ffn2_w1"], w["ffn2_w3"], w["ffn2_w2"])

        outs["shift_p"].append(jnp.take(zb, last_p, axis=0)[:, :B_COLS])
        outs["conv_p"].append(tail_p[:, CONV_HALO - (CONV_W - 1):])
        outs["mk_p"].append(mk.reshape(bp, N_MEM, X_HEADS, X_HEAD_DIM))
        outs["mv_p"].append(mv.reshape(bp, N_MEM, X_HEADS, X_HEAD_DIM))
        outs["shift_s"].append(jnp.take(zb, last_s, axis=0)[:, :B_COLS])
        outs["conv_s"].append(tail_s[:, CONV_HALO - (CONV_W - 1):])
        outs["gv_s"].append(v_n[n_p:].reshape(bs, ts, A_WIDTH))

    g_final = n_final.reshape(1, d)
    y_p = _final_norm(x, g_final, 0, n_p)
    y_s = _final_norm(x, g_final, n_p, n_s)
    st = lambda k: jnp.stack(outs[k])
    return (y_p.reshape(bp, tp, d), y_s.reshape(bs, ts, d),
            sp_all.reshape(depth, bp, B_HEADS, B_HEAD, B_HEAD), st("shift_p"), st("conv_p"), st("mk_p"), st("mv_p"),
            ss_all.reshape(depth, bs, B_HEADS, B_HEAD, B_HEAD), st("shift_s"), st("conv_s"), st("gv_s"))
```

```python
import functools
import math

import jax
import jax.numpy as jnp
from jax import lax
from jax.experimental import pallas as pl
from jax.experimental.pallas import tpu as pltpu

F32 = jnp.float32
BF16 = jnp.bfloat16

LANES = 128
SUBLANES = 8
VMEM_LIMIT_BYTES = 56 * 1024 * 1024

D_MODEL = 2048
A_WIDTH = 512
A_GROUPS = 4
CHUNK = 128
B_WIDTH = 1024
B_HEAD = 64
B_HEADS = 16
B_PAIRS = B_HEADS // 2
B_LORA = 64 + 64 + 160
B_COLS = 3 * B_WIDTH + B_LORA
RWKV_GN_EPS = 64e-5
RWKV_CHUNK = 64
C_WIDTH = 512
CONV_W = 31
CONV_HALO = 32
N_MEM = 256
X_HEADS = 4
X_HEAD_DIM = 128
X_WIDTH = 512
EPS = 1e-6
LN_EPS = 1e-5

ROW_TILE = 512
COL_TILE = 512
B_COLS_PAD = -(-B_COLS // COL_TILE) * COL_TILE
B_LORA_PAD = B_COLS_PAD - 3 * B_WIDTH

SCAN_MXU_PASSES = 1


def _params(*sem):
    return pltpu.CompilerParams(dimension_semantics=sem, vmem_limit_bytes=VMEM_LIMIT_BYTES)


def _layer_spec(l, block, index_fn):
    return pl.BlockSpec((None,) + tuple(block), lambda *g: (l,) + tuple(index_fn(*g)))


def _layer_vec(l, width):
    return _layer_spec(l, (1, width), lambda *g: (0, 0))


def _into_args(into):
    if into is None:
        return {"specs": [], "args": []}
    return {"specs": [pl.BlockSpec(memory_space=pl.ANY)], "args": [into]}


def _ignore_inputs(kernel_fn, start, count):
    def wrapped(*refs):
        return kernel_fn(*refs[:start], *refs[start + count:])
    return wrapped


def _rms(x, g):
    return x * lax.rsqrt(jnp.mean(x * x, axis=-1, keepdims=True) + EPS) * g


def _layernorm(x, g, b):
    xc = x - jnp.mean(x, axis=-1, keepdims=True)
    return xc * lax.rsqrt(jnp.mean(xc * xc, axis=-1, keepdims=True) + LN_EPS) * g + b


def _dot(a, b):
    return jnp.dot(a, b, preferred_element_type=F32)


def _dot_nt(a, b):
    return lax.dot_general(a, b, (((1,), (1,)), ((), ())), preferred_element_type=F32)


def _split2(x):
    hi = x.astype(BF16)
    lo = (x - hi.astype(F32)).astype(BF16)
    return hi, lo


def _split3(x):
    hi = x.astype(BF16)
    r1 = x - hi.astype(F32)
    mid = r1.astype(BF16)
    lo = (r1 - mid.astype(F32)).astype(BF16)
    return hi, mid, lo


def _split(x):
    return _split2(x) if SCAN_MXU_PASSES == 3 else (x.astype(BF16),)


def _dotp(a_parts, b_parts, dot=_dot):
    acc = dot(a_parts[0], b_parts[0])
    if len(a_parts) > 1:
        acc = acc + (dot(a_parts[0], b_parts[1]) + dot(a_parts[1], b_parts[0]))
    return acc


def _dot_sel(sel, x):
    sel = sel.astype(BF16)
    hi, mid, lo = _split3(x)
    return _dot(sel, hi) + (_dot(sel, mid) + _dot(sel, lo))


def _dot_sel_right(x, sel):
    hi, lo = _split2(x)
    return _dot(hi, sel) + _dot(lo, sel)


def _head_sum(x, ones_blk):
    cols = [_dot_sel_right(x[:, p * LANES:(p + 1) * LANES], ones_blk)
            for p in range(x.shape[1] // LANES)]
    return jnp.concatenate(cols, axis=1)


def _softplus(y):
    return jnp.maximum(y, 0.0) + jnp.log1p(jnp.exp(-jnp.abs(y)))


def _ffn_kernel(x_ref, g_ref, w1_ref, w3_ref, w2_ref, o_ref, h_sc, acc_sc):
    j = pl.program_id(1)

    @pl.when(j == 0)
    def _():
        h_sc[...] = _rms(x_ref[...], g_ref[...]).astype(BF16)
        acc_sc[...] = jnp.zeros_like(acc_sc)

    h = h_sc[...]
    a = _dot(h, w1_ref[...])
    b = _dot(h, w3_ref[...])
    act = (a * jax.nn.sigmoid(a) * b).astype(BF16)
    acc_sc[...] += _dot(act, w2_ref[...])

    @pl.when(j == pl.num_programs(1) - 1)
    def _():
        o_ref[...] = x_ref[...] + 0.5 * acc_sc[...]


def _ffn(x, l, g, w1, w3, w2):
    n, d = x.shape
    f = w1.shape[2]
    tm, tf = ROW_TILE, COL_TILE
    return pl.pallas_call(
        _ffn_kernel,
        out_shape=jax.ShapeDtypeStruct((n, d), F32),
        grid=(n // tm, f // tf),
        in_specs=[
            pl.BlockSpec((tm, d), lambda i, j: (i, 0)),
            _layer_vec(l, d),
            _layer_spec(l, (d, tf), lambda i, j: (0, j)),
            _layer_spec(l, (d, tf), lambda i, j: (0, j)),
            _layer_spec(l, (tf, d), lambda i, j: (j, 0)),
        ],
        out_specs=pl.BlockSpec((tm, d), lambda i, j: (i, 0)),
        scratch_shapes=[pltpu.VMEM((tm, d), BF16), pltpu.VMEM((tm, d), F32)],
        compiler_params=_params("parallel", "arbitrary"),
        name="ffn",
    )(x, g, w1, w3, w2)


def _norm_mm_kernel(x_ref, g_ref, w_ref, o_ref, h_sc):
    @pl.when(pl.program_id(1) == 0)
    def _():
        h_sc[...] = _rms(x_ref[...], g_ref[...]).astype(BF16)

    o_ref[...] = _dot(h_sc[...], w_ref[...])


def _norm_mm(x, l, g, w, tn):
    n, d = x.shape
    cols = w.shape[2]
    tm = min(ROW_TILE, n)
    return pl.pallas_call(
        _norm_mm_kernel,
        out_shape=jax.ShapeDtypeStruct((n, cols), F32),
        grid=(n // tm, cols // tn),
        in_specs=[
            pl.BlockSpec((tm, d), lambda i, j: (i, 0)),
            _layer_vec(l, d),
            _layer_spec(l, (d, tn), lambda i, j: (0, j)),
        ],
        out_specs=pl.BlockSpec((tm, tn), lambda i, j: (i, j)),
        scratch_shapes=[pltpu.VMEM((tm, d), BF16)],
        compiler_params=_params("parallel", "arbitrary"),
        name="norm_mm",
    )(x, g, w)


def _gmlp_kernel(x_ref, g_ref, w_ref, lng_ref, lnb_ref, ws_ref, bs_ref, act_ref, vn_ref, *,
                 n_prompt_tiles, short_len):
    is_sample = pl.program_id(0) >= n_prompt_tiles
    h = _rms(x_ref[...], g_ref[...]).astype(BF16)
    z = jax.nn.gelu(_dot(h, w_ref[...]))
    u = z[:, :A_WIDTH]
    v_n = _layernorm(z[:, A_WIDTH:], lng_ref[...], lnb_ref[...])
    vn_ref[...] = v_n

    row = lax.broadcasted_iota(jnp.int32, (CHUNK, CHUNK), 0)
    col = lax.broadcasted_iota(jnp.int32, (CHUNK, CHUNK), 1)
    shift = jnp.where(is_sample, int(math.log2(short_len)), int(math.log2(CHUNK)))
    shift = jnp.broadcast_to(shift.astype(jnp.int32), (CHUNK, CHUNK))
    same_seq = lax.shift_right_logical(row, shift) == lax.shift_right_logical(col, shift)
    keep = (col <= row) & same_seq
    v_b = v_n.astype(BF16)
    tm = x_ref.shape[0]
    gw = A_WIDTH // A_GROUPS
    for grp in range(A_GROUPS):
        ws = jnp.where(keep, ws_ref[0, grp], 0.0).astype(BF16)
        bias = bs_ref[0, grp]
        for c in range(tm // CHUNK):
            rows = slice(c * CHUNK, (c + 1) * CHUNK)
            cols = slice(grp * gw, (grp + 1) * gw)
            mixed = _dot(ws, v_b[rows, cols]) + bias
            act_ref[rows, cols] = (u[rows, cols] * mixed).astype(BF16)


def _gmlp(x, l, g, w_a, ln_g, ln_b, ws_all, bs_all, n_prompt, short_len):
    n, d = x.shape
    tm = ROW_TILE
    npt = n_prompt // tm
    kern = functools.partial(_gmlp_kernel, n_prompt_tiles=npt, short_len=short_len)
    sel = lambda i: (jnp.where(i >= npt, 1, 0), 0, 0, 0)
    return pl.pallas_call(
        kern,
        out_shape=(jax.ShapeDtypeStruct((n, A_WIDTH), BF16), jax.ShapeDtypeStruct((n, A_WIDTH), F32)),
        grid=(n // tm,),
        in_specs=[
            pl.BlockSpec((tm, d), lambda i: (i, 0)),
            _layer_vec(l, d),
            _layer_spec(l, (d, 2 * A_WIDTH), lambda i: (0, 0)),
            _layer_vec(l, A_WIDTH),
            _layer_vec(l, A_WIDTH),
            _layer_spec(l, (1, A_GROUPS, CHUNK, CHUNK), sel),
            _layer_spec(l, (1, A_GROUPS, CHUNK, CHUNK), sel),
        ],
        out_specs=(pl.BlockSpec((tm, A_WIDTH), lambda i: (i, 0)),
                   pl.BlockSpec((tm, A_WIDTH), lambda i: (i, 0))),
        compiler_params=_params("parallel"),
        name="gmlp",
    )(x, g, w_a, ln_g, ln_b, ws_all, bs_all)


def _rwkv_prep_kernel(z_ref, prev_ref, mu_ref, w0_ref, a0_ref, kk_ref, ka_ref, rk_ref, wl_ref,
                      r_ref, kkn_ref, b_ref, km_ref, v_ref, ld_ref, g_ref, bonus_ref, *,
                      seq_len):
    tc = z_ref.shape[0]
    z = z_ref[...]
    row = lax.broadcasted_iota(jnp.int32, z.shape, 0)
    rolled = pltpu.roll(z, 1, 0)
    if seq_len >= tc:
        tiles_per_seq = seq_len // tc
        at_start = (pl.program_id(0) % tiles_per_seq) == 0
        last = prev_ref[SUBLANES - 1:SUBLANES, :]
        first = jnp.where(at_start, jnp.zeros_like(last), last)
        prev = jnp.where(row == 0, first, rolled)
    else:
        prev = jnp.where(row % seq_len == 0, prev_ref[...], rolled)
    zs = z + (prev - z) * mu_ref[...]

    r = zs[:, :B_WIDTH]
    k = zs[:, B_WIDTH:2 * B_WIDTH]
    v = zs[:, 2 * B_WIDTH:3 * B_WIDTH]
    lo0 = zs[:, 3 * B_WIDTH:3 * B_WIDTH + LANES]
    lane = lax.broadcasted_iota(jnp.int32, lo0.shape, 1)
    lo0 = jnp.where(lane < 64, jnp.tanh(lo0), lo0)
    lo_rest = jax.nn.sigmoid(zs[:, 3 * B_WIDTH + LANES:])
    lora_in = jnp.concatenate([lo0, lo_rest], axis=1).astype(BF16)
    lora = _dot(lora_in, wl_ref[...])
    lw = lora[:, :B_WIDTH]
    la = lora[:, B_WIDTH:2 * B_WIDTH]
    g = lora[:, 2 * B_WIDTH:]

    w_raw = -_softplus(-(w0_ref[...] + lw)) - 0.5
    ld = -jnp.exp(w_raw)
    a = jax.nn.sigmoid(a0_ref[...] + la)

    rl = lax.broadcasted_iota(jnp.int32, (LANES, LANES), 0)
    cl = lax.broadcasted_iota(jnp.int32, (LANES, LANES), 1)
    ones_blk = jnp.where((rl // B_HEAD) == (cl // B_HEAD), 1.0, 0.0).astype(BF16)

    kk = k * kk_ref[...]
    nrm = jnp.maximum(jnp.sqrt(_head_sum(kk * kk, ones_blk)), 1e-12)
    kk = kk / nrm
    k_mod = k * (1.0 + (a - 1.0) * ka_ref[...])
    bonus = _head_sum(r * k_mod * rk_ref[...], ones_blk) * v

    r_ref[...] = r
    kkn_ref[...] = kk
    b_ref[...] = kk * a
    km_ref[...] = k_mod
    v_ref[...] = v
    ld_ref[...] = ld
    g_ref[...] = g
    bonus_ref[...] = bonus


def _rwkv_prep(zb, prev, l, w, row0, nrows, seq_len, tc):
    t0 = row0 // tc
    if seq_len >= tc:
        per8 = tc // SUBLANES
        prev_spec = pl.BlockSpec(
            (SUBLANES, B_COLS_PAD), lambda i: (jnp.maximum((i + t0) * per8 - 1, 0), 0))
        prev_arr = zb
    else:
        prev_spec = pl.BlockSpec((tc, B_COLS_PAD), lambda i: (i, 0))
        prev_arr = prev
    out = jax.ShapeDtypeStruct((nrows, B_WIDTH), F32)
    return pl.pallas_call(
        functools.partial(_rwkv_prep_kernel, seq_len=seq_len),
        out_shape=(out,) * 8,
        grid=(nrows // tc,),
        in_specs=[
            pl.BlockSpec((tc, B_COLS_PAD), lambda i: (i + t0, 0)),
            prev_spec,
            _layer_vec(l, B_COLS_PAD), _layer_vec(l, B_WIDTH), _layer_vec(l, B_WIDTH),
            _layer_vec(l, B_WIDTH), _layer_vec(l, B_WIDTH), _layer_vec(l, B_WIDTH),
            _layer_spec(l, (B_LORA_PAD, 3 * B_WIDTH), lambda i: (0, 0)),
        ],
        out_specs=(pl.BlockSpec((tc, B_WIDTH), lambda i: (i, 0)),) * 8,
        compiler_params=_params("parallel"),
        name="rwkv_prep",
    )(zb, prev_arr, w["mu_b"], w["w0"], w["a0"], w["k_k"], w["k_a"], w["r_k"], w["w_lora"])


def _rwkv_chunk_kernel(r_ref, kk_ref, b_ref, km_ref, v_ref, ld_ref, g_ref, bonus_ref,
                       gng_ref, gnb_ref, s0_ref, act_ref, sout_ref, s_sc, *, nseq, zero_init):
    c = pl.program_id(1)
    C = RWKV_CHUNK
    tq = C // nseq
    rounds = int(math.log2(tq)) + 1
    pairs = range(B_PAIRS)
    lanes = lambda p: slice(p * LANES, (p + 1) * LANES)

    @pl.when(c == 0)
    def _():
        if zero_init:
            s_sc[...] = jnp.zeros_like(s_sc)
        else:
            top = lax.broadcasted_iota(jnp.int32, (LANES, B_HEAD), 0) < B_HEAD
            for s in range(nseq):
                for p in pairs:
                    sc = s0_ref[s, p]
                    s_sc[s, p] = jnp.concatenate(
                        [jnp.where(top, sc, 0.0), jnp.where(top, 0.0, sc)], axis=1)

    row = lax.broadcasted_iota(jnp.int32, (2 * C, 2 * C), 0)
    col = lax.broadcasted_iota(jnp.int32, (2 * C, 2 * C), 1)
    t_i = row % C
    j_i = col % C
    mask_n = ((t_i // tq) == (j_i // tq)) & (j_i < t_i) & (row < C)
    blockdiag = (row // B_HEAD) == (col // B_HEAD)
    head0 = lax.broadcasted_iota(jnp.int32, (2 * C, LANES), 1) < B_HEAD
    t_h = lax.broadcasted_iota(jnp.int32, (C, 2 * C), 0)
    col_h = lax.broadcasted_iota(jnp.int32, (C, 2 * C), 1)
    j_h = col_h % C
    same_h = (t_h // tq) == (j_h // tq)
    mask_r = same_h & (j_h <= t_h)
    head0_h = lax.broadcasted_iota(jnp.int32, (C, LANES), 1) < B_HEAD

    ld = ld_ref[...]
    ld_pad = jnp.concatenate([ld, jnp.zeros_like(ld)], axis=0)
    tri = jnp.where(mask_r & (col_h < C), 1.0, 0.0)
    cum = _dot_sel(tri, ld_pad)
    if nseq == 1:
        cum_end = jnp.broadcast_to(cum[C - 1:C, :], cum.shape)
    else:
        cum_end = _dot_sel(jnp.where(same_h & (col_h < C), 1.0, 0.0), ld_pad)
    p_t = jnp.exp(cum)
    p_prev = jnp.exp(cum - ld)
    p_inv = jnp.exp(-cum)
    p_tail = jnp.exp(cum_end - cum)
    p_end = jnp.exp(cum_end)

    kk = kk_ref[...]
    b = b_ref[...]
    km = km_ref[...]
    vv = v_ref[...]
    am = -kk * p_prev
    rm = r_ref[...] * p_t
    x = jnp.concatenate([am, rm], axis=0)
    y = jnp.concatenate([b * p_inv, km * p_inv], axis=0)
    zt = jnp.concatenate([b * p_tail, km * p_tail], axis=0)


    x0, r0 = [], []
    for p in pairs:
        xs_parts, rs_parts = [], []
        for s in range(nseq):
            rows = slice(s * tq, (s + 1) * tq)
            xs = jnp.concatenate([am[rows, lanes(p)], rm[rows, lanes(p)]], axis=0)
            xs0 = _dotp(_split(xs), _split(s_sc[s, p]), dot=_dot_nt)
            xs_parts.append(xs0[:tq])
            rs_parts.append(xs0[tq:])
        x0.append(xs_parts[0] if nseq == 1 else jnp.concatenate(xs_parts, axis=0))
        r0.append(rs_parts[0] if nseq == 1 else jnp.concatenate(rs_parts, axis=0))

    chains = [(p, hd) for p in pairs for hd in range(2)]
    nmat, mr, w = {}, {}, {}
    for p in pairs:
        y_parts = _split(y[:, lanes(p)])
        for hd in range(2):
            xm = jnp.where(head0 if hd == 0 else ~head0, x[:, lanes(p)], 0.0)
            gm = _dotp(_split(xm), y_parts, dot=_dot_nt)
            nmat[p, hd] = jnp.where(mask_n, gm, 0.0)
            mr[p, hd] = jnp.where(mask_r, gm[C:], 0.0)
            w[p, hd] = jnp.concatenate([x0[p], vv[:, lanes(p)]], axis=0)

    zero_rows = jnp.zeros((C, 2 * C), F32)
    for it in range(rounds):
        last = it + 1 == rounds
        for ch in chains:
            n_parts = _split(nmat[ch])
            n_top = tuple(a[:C] for a in n_parts)
            w_parts = _split(w[ch])
            if last:
                upd = _dotp(n_top, w_parts)
            else:
                rhs = tuple(jnp.concatenate([a, bb], axis=1) for a, bb in zip(n_parts, w_parts))
                out = _dotp(n_top, rhs)
                nmat[ch] = jnp.concatenate([out[:, :2 * C], zero_rows], axis=0)
                upd = out[:, 2 * C:]
            w[ch] = jnp.concatenate([w[ch][:C] + upd, w[ch][C:]], axis=0)

    o_parts, w_pair = [], []
    for p in pairs:
        oh = [_dotp(_split(mr[p, hd]), _split(w[p, hd])) for hd in range(2)]
        o_parts.append(r0[p] + jnp.where(head0_h, oh[0], oh[1]))
        w_pair.append(jnp.where(head0, w[p, 0], w[p, 1]))

    for p in pairs:
        for s in range(nseq):
            rows = slice(s * tq, (s + 1) * tq)
            ws = jnp.concatenate([w_pair[p][:C][rows], vv[rows, lanes(p)]], axis=0)
            zs = jnp.concatenate([zt[:C][rows, lanes(p)], zt[C:][rows, lanes(p)]], axis=0)
            if 2 * tq < LANES:
                pad = jnp.zeros((LANES - 2 * tq, LANES), F32)
                ws = jnp.concatenate([ws, pad], axis=0)
                zs = jnp.concatenate([zs, pad], axis=0)
            upd = _dotp(_split(ws.T), _split(zs))
            pe = p_end[s * tq:s * tq + 1, lanes(p)]
            s_sc[s, p] = s_sc[s, p] * pe + jnp.where(blockdiag, upd, 0.0)

    @pl.when(c == pl.num_programs(1) - 1)
    def _():
        top = lax.broadcasted_iota(jnp.int32, (LANES, LANES), 0) < B_HEAD
        for s in range(nseq):
            for p in pairs:
                full = s_sc[s, p]
                sout_ref[s, p] = jnp.where(top, full, pltpu.roll(full, B_HEAD, 1))[:, :B_HEAD]

    o = jnp.concatenate(o_parts, axis=1)
    ones_blk = jnp.where(blockdiag, 1.0, 0.0).astype(BF16)
    mean = _head_sum(o, ones_blk) * (1.0 / B_HEAD)
    oc = o - mean
    var = _head_sum(oc * oc, ones_blk) * (1.0 / B_HEAD)
    on = oc * lax.rsqrt(var + RWKV_GN_EPS) * gng_ref[...] + gnb_ref[...] + bonus_ref[...]
    act_ref[...] = (on * g_ref[...]).astype(BF16)


def _rwkv_chunk(prep, l, w, s0, s_all, nseq, seq_len, row0, total_rows, act_into=None):
    r, kk, b, km, v, ld, g, bonus = prep
    rows = r.shape[0]
    C = RWKV_CHUNK
    n_seqs = rows // seq_len
    n_groups = n_seqs // nseq
    n_chunks = nseq * seq_len // C
    t0 = row0 // C
    n_layers = w["gn_g"].shape[0]
    tok = pl.BlockSpec((C, B_WIDTH), lambda gi, c: (gi * n_chunks + c, 0))
    st = _layer_spec(l, (nseq, B_PAIRS, LANES, B_HEAD), lambda gi, c: (gi, 0, 0, 0))
    zero_init = s0 is None
    if zero_init:
        s0 = jnp.zeros((1, nseq, B_PAIRS, LANES, B_HEAD), F32)
        s0_spec = pl.BlockSpec((None, nseq, B_PAIRS, LANES, B_HEAD), lambda gi, c: (0, 0, 0, 0, 0))
    else:
        s0_spec = st
    kern = functools.partial(_rwkv_chunk_kernel, nseq=nseq, zero_init=zero_init)
    operands = [r, kk, b, km, v, ld, g, bonus, w["gn_g"], w["gn_b"], s0]
    in_specs = [tok] * 8 + [_layer_vec(l, B_WIDTH), _layer_vec(l, B_WIDTH), s0_spec]
    n_in = len(operands)
    aliases = {}
    for out_idx, into in ((0, act_into), (1, s_all)):
        if into is not None:
            aliases[len(operands)] = out_idx
            operands.append(into)
            in_specs.append(pl.BlockSpec(memory_space=pl.ANY))
    kern = _ignore_inputs(kern, n_in, len(aliases))
    return pl.pallas_call(
        kern,
        out_shape=(jax.ShapeDtypeStruct((total_rows, B_WIDTH), BF16),
                   jax.ShapeDtypeStruct((n_layers, n_seqs, B_PAIRS, LANES, B_HEAD), F32)),
        grid=(n_groups, n_chunks),
        in_specs=in_specs,
        out_specs=(pl.BlockSpec((C, B_WIDTH), lambda gi, c: (t0 + gi * n_chunks + c, 0)), st),
        scratch_shapes=[pltpu.VMEM((nseq, B_PAIRS, LANES, LANES), F32)],
        input_output_aliases=aliases,
        compiler_params=_params("parallel", "arbitrary"),
        name="rwkv_chunk",
    )(*operands)


def _conv_kernel(z_ref, buf_ref, dw_ref, dwb_ref, lng_ref, lnb_ref, act_ref, tail_ref, ext_sc, *,
                 nseq):
    tc = z_ref.shape[0] // nseq
    H = CONV_HALO
    span = H + tc

    @pl.when(pl.program_id(1) == 0)
    def _():
        for s in range(nseq):
            ext_sc[s * span:s * span + H, :] = buf_ref[s]

    z = z_ref[...]
    glu = z[:, :C_WIDTH] * jax.nn.sigmoid(z[:, C_WIDTH:])
    off = H - (CONV_W - 1)
    accs = []
    for s in range(nseq):
        base = s * span
        ext_sc[base + H:base + span, :] = glu[s * tc:(s + 1) * tc]
        acc = jnp.zeros((tc, C_WIDTH), F32)
        for j in range(CONV_W):
            acc = acc + dw_ref[j:j + 1, :] * ext_sc[base + off + j:base + off + j + tc, :]
        accs.append(acc)
        tail = ext_sc[base + tc:base + span, :]
        tail_ref[s] = tail
        ext_sc[base:base + H, :] = tail
    acc = accs[0] if nseq == 1 else jnp.concatenate(accs, axis=0)
    y = _layernorm(acc + dwb_ref[...], lng_ref[...], lnb_ref[...])
    act_ref[...] = (y * jax.nn.sigmoid(y)).astype(BF16)


def _conv(zc, buf, l, w, row0, n_seqs, seq_len, tc, nseq=1, into=None):
    tiles = seq_len // tc
    tc = tc * nseq
    t0 = row0 // tc
    vec = _layer_vec(l, C_WIDTH)
    extra = _into_args(into)
    return pl.pallas_call(
        _ignore_inputs(functools.partial(_conv_kernel, nseq=nseq), 6, len(extra["args"])),
        out_shape=(jax.ShapeDtypeStruct((zc.shape[0], C_WIDTH), BF16),
                   jax.ShapeDtypeStruct((n_seqs, CONV_HALO, C_WIDTH), F32)),
        grid=(n_seqs // nseq, tiles),
        in_specs=[
            pl.BlockSpec((tc, 2 * C_WIDTH), lambda s, i: (t0 + s * tiles + i, 0)),
            pl.BlockSpec((nseq, CONV_HALO, C_WIDTH), lambda s, i: (s, 0, 0)),
            _layer_spec(l, (CONV_HALO, C_WIDTH), lambda s, i: (0, 0)),
            vec, vec, vec,
        ] + extra["specs"],
        out_specs=(pl.BlockSpec((tc, C_WIDTH), lambda s, i: (t0 + s * tiles + i, 0)),
                   pl.BlockSpec((nseq, CONV_HALO, C_WIDTH), lambda s, i: (s, 0, 0))),
        scratch_shapes=[pltpu.VMEM((nseq * CONV_HALO + tc, C_WIDTH), F32)],
        input_output_aliases={6: 0} if into is not None else {},
        compiler_params=_params("parallel", "arbitrary"),
        name="conv",
    )(zc, buf, w["dw"], w["dw_b"], w["ln_c_g"], w["ln_c_b"], *extra["args"])


def _merge_kernel(x_ref, g_ref, wg0_ref, wg1_ref, wg2_ref, aa_ref, ab_ref, ac_ref,
                  woa_ref, wob_ref, woc_ref, wout_ref, o_ref, h_sc, acc_sc):
    j = pl.program_id(1)

    @pl.when(j == 0)
    def _():
        h_sc[...] = _rms(x_ref[...], g_ref[...]).astype(BF16)
        acc_sc[...] = jnp.zeros_like(acc_sc)

    h = h_sc[...]
    merged = jax.nn.sigmoid(_dot(h, wg0_ref[...])) * _dot(aa_ref[...], woa_ref[...])
    merged += jax.nn.sigmoid(_dot(h, wg1_ref[...])) * _dot(ab_ref[...], wob_ref[...])
    merged += jax.nn.sigmoid(_dot(h, wg2_ref[...])) * _dot(ac_ref[...], woc_ref[...])
    acc_sc[...] += _dot(merged.astype(BF16), wout_ref[...])

    @pl.when(j == pl.num_programs(1) - 1)
    def _():
        o_ref[...] = x_ref[...] + acc_sc[...]


def _merge(x, l, w, act_a, act_b, act_c):
    n, d = x.shape
    tm, tn = ROW_TILE, COL_TILE
    nj = d // tn
    row = lambda width: pl.BlockSpec((tm, width), lambda i, j: (i, 0))
    colw = lambda k: _layer_spec(l, (k, tn), lambda i, j: (0, j))
    gate = lambda k: _layer_spec(l, (d, tn), lambda i, j: (0, k * nj + j))
    return pl.pallas_call(
        _merge_kernel,
        out_shape=jax.ShapeDtypeStruct((n, d), F32),
        grid=(n // tm, nj),
        in_specs=[
            row(d),
            _layer_vec(l, d),
            gate(0), gate(1), gate(2),
            row(A_WIDTH), row(B_WIDTH), row(C_WIDTH),
            colw(A_WIDTH), colw(B_WIDTH), colw(C_WIDTH),
            _layer_spec(l, (tn, d), lambda i, j: (j, 0)),
        ],
        out_specs=row(d),
        scratch_shapes=[pltpu.VMEM((tm, d), BF16), pltpu.VMEM((tm, d), F32)],
        compiler_params=_params("parallel", "arbitrary"),
        name="merge",
    )(x, w["n_mix"], w["w_g"], w["w_g"], w["w_g"], act_a, act_b, act_c,
      w["wo_a"], w["wo_b"], w["wo_c"], w["w_out"])


def _xattn_kernel(x_ref, g_ref, wq_ref, k_ref, v_ref, wo_ref, o_ref, *, nseq, head_rows):
    x = x_ref[...]
    tq = x.shape[0]
    rows_per_seq = tq // nseq
    h = _rms(x, g_ref[...]).astype(BF16)
    q = _dot(h, wq_ref[...])
    scale = X_HEAD_DIM ** -0.5

    def memory(ref, s, hd):
        if head_rows:
            return ref[pl.ds(s * N_MEM * X_HEADS + hd, N_MEM, stride=X_HEADS), :].astype(BF16)
        return ref[s, :, hd * X_HEAD_DIM:(hd + 1) * X_HEAD_DIM].astype(BF16)

    units = [(s, hd) for s in range(nseq) for hd in range(X_HEADS)]
    scores = []
    for s, hd in units:
        qh = q[s * rows_per_seq:(s + 1) * rows_per_seq, hd * X_HEAD_DIM:(hd + 1) * X_HEAD_DIM]
        scores.append(_dot_nt(qh.astype(BF16), memory(k_ref, s, hd)) * scale)
    sc = jnp.concatenate(scores, axis=0)
    e = jnp.exp(sc - jnp.max(sc, axis=-1, keepdims=True))
    pr = e / jnp.sum(e, axis=-1, keepdims=True)
    outs = [_dot(pr[u * rows_per_seq:(u + 1) * rows_per_seq].astype(BF16), memory(v_ref, s, hd))
            for u, (s, hd) in enumerate(units)]
    seq_out = [jnp.concatenate(outs[s * X_HEADS:(s + 1) * X_HEADS], axis=1) for s in range(nseq)]
    o = seq_out[0] if nseq == 1 else jnp.concatenate(seq_out, axis=0)
    o_ref[...] = x + _dot(o.astype(BF16), wo_ref[...])


def _xattn(x, l, w, mem_k, mem_v, row0, nrows, seq_len, tq, kv_layer=None):
    d = x.shape[1]
    t0 = row0 // tq
    if seq_len >= tq:
        nseq = 1
        per_seq = seq_len // tq
        seq_of = lambda i: i // per_seq
    else:
        nseq = tq // seq_len
        seq_of = lambda i: i
    if kv_layer is None:
        kv = pl.BlockSpec((nseq, N_MEM, X_WIDTH), lambda i: (seq_of(i), 0, 0))
    else:
        steps = nrows // tq
        kv = pl.BlockSpec((nseq * N_MEM * X_HEADS, X_HEAD_DIM),
                          lambda i: (kv_layer * steps + seq_of(i), 0))
    return pl.pallas_call(
        functools.partial(_xattn_kernel, nseq=nseq, head_rows=kv_layer is not None),
        out_shape=jax.ShapeDtypeStruct(x.shape, F32),
        grid=(nrows // tq,),
        in_specs=[
            pl.BlockSpec((tq, d), lambda i: (i + t0, 0)),
            _layer_vec(l, d),
            _layer_spec(l, (d, X_WIDTH), lambda i: (0, 0)),
            kv, kv,
            _layer_spec(l, (X_WIDTH, d), lambda i: (0, 0)),
        ],
        out_specs=pl.BlockSpec((tq, d), lambda i: (i + t0, 0)),
        input_output_aliases={0: 0},
        compiler_params=_params("parallel"),
        name="xattn",
    )(x, w["n_x"], w["w_xq"], mem_k, mem_v, w["w_xo"])


def _final_norm_kernel(x_ref, g_ref, o_ref):
    o_ref[...] = _rms(x_ref[...], g_ref[...])


def _final_norm(x, g, row0, nrows):
    d = x.shape[1]
    tm = ROW_TILE
    t0 = row0 // tm
    return pl.pallas_call(
        _final_norm_kernel,
        out_shape=jax.ShapeDtypeStruct((nrows, d), F32),
        grid=(nrows // tm,),
        in_specs=[pl.BlockSpec((tm, d), lambda i: (i + t0, 0)), pl.BlockSpec((1, d), lambda i: (0, 0))],
        out_specs=pl.BlockSpec((tm, d), lambda i: (i, 0)),
        compiler_params=_params("parallel"),
        name="final_norm",
    )(x, g)


def _prepare_weights(w, ts):
    bf = lambda a: a.astype(BF16)
    vec = lambda a: a.reshape(a.shape[0], 1, -1)
    depth = w["w_in"].shape[0]
    a_cols = 2 * A_WIDTH
    c0 = a_cols + B_COLS
    g0 = c0 + 2 * C_WIDTH
    w_in = w["w_in"]
    p = {}
    for name in ("n_ffn1", "n_mix", "n_x", "n_mem", "n_ffn2", "w0", "a0", "k_k", "k_a", "r_k",
                 "gn_g", "gn_b", "ln_a_g", "ln_a_b", "dw_b", "ln_c_g", "ln_c_b"):
        p[name] = vec(w[name])
    for name in ("ffn1_w1", "ffn1_w3", "ffn1_w2", "ffn2_w1", "ffn2_w3", "ffn2_w2", "wo_a", "wo_b",
                 "wo_c", "w_out", "w_xq", "w_xo"):
        p[name] = bf(w[name])
    p["w_a"] = bf(w_in[:, :, :a_cols])
    p["w_b"] = bf(jnp.pad(w_in[:, :, a_cols:c0], ((0, 0), (0, 0), (0, B_COLS_PAD - B_COLS))))
    p["w_c"] = bf(w_in[:, :, c0:g0])
    p["w_g"] = bf(w_in[:, :, g0:])
    p["w_kv"] = bf(jnp.concatenate([w["w_xk"], w["w_xv"]], axis=2))
    p["mu_b"] = vec(jnp.pad(w["mu_b"], ((0, 0), (0, B_COLS_PAD - B_COLS))))
    wl = jnp.zeros((depth, B_LORA_PAD, 3 * B_WIDTH), F32)
    wl = wl.at[:, 0:64, 0:B_WIDTH].set(w["w2_decay"])
    wl = wl.at[:, 64:128, B_WIDTH:2 * B_WIDTH].set(w["a2_aaa"])
    wl = wl.at[:, 128:B_LORA, 2 * B_WIDTH:].set(w["g2_gate"])
    p["w_lora"] = bf(wl)
    p["dw"] = jnp.pad(w["dw"], ((0, 0), (0, CONV_HALO - CONV_W), (0, 0)))
    reps = CHUNK // ts
    w_s, b_s = w["w_s"], w["b_s"]
    p["ws_all"] = jnp.stack([w_s, jnp.tile(w_s[:, :, :ts, :ts], (1, 1, reps, reps))], axis=1)
    bias_s = jnp.tile(b_s[:, :, :ts], (1, 1, reps))
    full = (depth, A_GROUPS, CHUNK, CHUNK)
    p["bs_all"] = jnp.stack([jnp.broadcast_to(b_s[..., None], full),
                             jnp.broadcast_to(bias_s[..., None], full)], axis=1)
    return p


def kernel(x_prompt, x_sample, mem_prompt, state_rwkv, state_shift, state_conv, cache_mem_k, cache_mem_v, n_ffn1, ffn1_w1, ffn1_w3, ffn1_w2, n_mix, w_in, mu_b, w0, w2_decay, a0, a2_aaa, g2_gate, k_k, k_a, r_k, gn_g, gn_b, wo_b, ln_a_g, ln_a_b, w_s, b_s, wo_a, dw, dw_b, ln_c_g, ln_c_b, wo_c, w_out, n_x, n_mem, w_xq, w_xk, w_xv, w_xo, n_ffn2, ffn2_w1, ffn2_w3, ffn2_w2, n_final):
    weights = dict(
        n_ffn1=n_ffn1, ffn1_w1=ffn1_w1, ffn1_w3=ffn1_w3, ffn1_w2=ffn1_w2, n_mix=n_mix, w_in=w_in,
        mu_b=mu_b, w0=w0, w2_decay=w2_decay, a0=a0, a2_aaa=a2_aaa, g2_gate=g2_gate, k_k=k_k,
        k_a=k_a, r_k=r_k, gn_g=gn_g, gn_b=gn_b, wo_b=wo_b, ln_a_g=ln_a_g, ln_a_b=ln_a_b,
        w_s=w_s, b_s=b_s, wo_a=wo_a, dw=dw, dw_b=dw_b, ln_c_g=ln_c_g, ln_c_b=ln_c_b, wo_c=wo_c,
        w_out=w_out, n_x=n_x, n_mem=n_mem, w_xq=w_xq, w_xk=w_xk, w_xv=w_xv, w_xo=w_xo,
        n_ffn2=n_ffn2, ffn2_w1=ffn2_w1, ffn2_w3=ffn2_w3, ffn2_w2=ffn2_w2)
    depth = w_in.shape[0]
    bp, tp, d = x_prompt.shape
    bs, ts, _ = x_sample.shape
    n_p, n_s = bp * tp, bs * ts
    n = n_p + n_s
    w = _prepare_weights(weights, ts)
    x = jnp.concatenate([x_prompt.reshape(n_p, d), x_sample.reshape(n_s, d)], axis=0)
    mem = mem_prompt.reshape(bp * N_MEM, d)
    cache_k = cache_mem_k.reshape(depth * bs * N_MEM * X_HEADS, X_HEAD_DIM)
    cache_v = cache_mem_v.reshape(depth * bs * N_MEM * X_HEADS, X_HEAD_DIM)
    s0_sample = state_rwkv.reshape(depth, bs, B_PAIRS, 2 * B_HEAD, B_HEAD)
    pad_buf = lambda b: jnp.pad(b, ((0, 0), (CONV_HALO - (CONV_W - 1), 0), (0, 0)))
    zero_buf = jnp.zeros((bp, CONV_HALO, C_WIDTH), F32)
    last_p = jnp.arange(bp, dtype=jnp.int32) * tp + (tp - 1)
    last_s = n_p + jnp.arange(bs, dtype=jnp.int32) * ts + (ts - 1)

    outs = {k: [] for k in ("shift_p", "conv_p", "mk_p", "mv_p", "shift_s", "conv_s", "gv_s")}
    sp_all = jnp.zeros((depth, bp, B_PAIRS, 2 * B_HEAD, B_HEAD), F32)
    ss_all = jnp.zeros((depth, bs, B_PAIRS, 2 * B_HEAD, B_HEAD), F32)
    act_b = jnp.zeros((n, B_WIDTH), BF16)
    act_c = jnp.zeros((n, C_WIDTH), BF16)
    for l in range(depth):
        kv = _norm_mm(mem, l, w["n_mem"], w["w_kv"], 2 * X_WIDTH)
        mk = kv[:, :X_WIDTH].reshape(bp, N_MEM, X_WIDTH)
        mv = kv[:, X_WIDTH:].reshape(bp, N_MEM, X_WIDTH)

        x = _ffn(x, l, w["n_ffn1"], w["ffn1_w1"], w["ffn1_w3"], w["ffn1_w2"])

        act_a, v_n = _gmlp(x, l, w["n_mix"], w["w_a"], w["ln_a_g"], w["ln_a_b"],
                           w["ws_all"], w["bs_all"], n_p, ts)
        zb = _norm_mm(x, l, w["n_mix"], w["w_b"], B_COLS_PAD // 2)
        zc = _norm_mm(x, l, w["n_mix"], w["w_c"], 2 * C_WIDTH)

        shift_exp = jnp.pad(state_shift[l][:, None, :],
                            ((0, 0), (0, ts - 1), (0, B_COLS_PAD - B_COLS))).reshape(n_s, B_COLS_PAD)
        prep_p = _rwkv_prep(zb, None, l, w, 0, n_p, tp, 256)
        prep_s = _rwkv_prep(zb, shift_exp, l, w, n_p, n_s, ts, 256)
        act_b, sp_all = _rwkv_chunk(prep_p, l, w, None, sp_all, 1, tp, 0, n, act_into=act_b)
        act_b, ss_all = _rwkv_chunk(prep_s, l, w, s0_sample, ss_all, RWKV_CHUNK // ts, ts, n_p, n,
                                    act_into=act_b)

        act_c, tail_p = _conv(zc, zero_buf, l, w, 0, bp, tp, 256, into=act_c)
        act_c, tail_s = _conv(zc, pad_buf(state_conv[l]), l, w, n_p, bs, ts, ts, nseq=8, into=act_c)

        x = _merge(x, l, w, act_a, act_b, act_c)

        x = _xattn(x, l, w, mk, mv, 0, n_p, tp, ROW_TILE)
        x = _xattn(x, l, w, cache_k, cache_v, n_p, n_s, ts, 64, kv_layer=l)

        x = _ffn(x, l, w["n_ffn2"], w["ffn2_w1"], w["ffn2_w3"], w["ffn2_w2"])

        outs["shift_p"].append(jnp.take(zb, last_p, axis=0)[:, :B_COLS])
        outs["conv_p"].append(tail_p[:, CONV_HALO - (CONV_W - 1):])
        outs["mk_p"].append(mk.reshape(bp, N_MEM, X_HEADS, X_HEAD_DIM))
        outs["mv_p"].append(mv.reshape(bp, N_MEM, X_HEADS, X_HEAD_DIM))
        outs["shift_s"].append(jnp.take(zb, last_s, axis=0)[:, :B_COLS])
        outs["conv_s"].append(tail_s[:, CONV_HALO - (CONV_W - 1):])
        outs["gv_s"].append(v_n[n_p:].reshape(bs, ts, A_WIDTH))

    g_final = n_final.reshape(1, d)
    y_p = _final_norm(x, g_final, 0, n_p)
    y_s = _final_norm(x, g_final, n_p, n_s)
    st = lambda k: jnp.stack(outs[k])
    return (y_p.reshape(bp, tp, d), y_s.reshape(bs, ts, d),
            sp_all.reshape(depth, bp, B_HEADS, B_HEAD, B_HEAD), st("shift_p"), st("conv_p"), st("mk_p"), st("mv_p"),
            ss_all.reshape(depth, bs, B_HEADS, B_HEAD, B_HEAD), st("shift_s"), st("conv_s"), st("gv_s"))
```
